```python
import math
import jax, jax.numpy as jnp
from jax import lax
import numpy as np

D_MODEL = 1024
BATCH = 4
SEQ = 8192
DEPTH = 4

N_MIXERS = 4
HEAD_DIM = 64
GROUP_W = D_MODEL // N_MIXERS
HEADS_PER_MIXER = GROUP_W // HEAD_DIM
D_MIX = N_MIXERS * GROUP_W
N_CHUNKS = 13
D_IN = N_CHUNKS * GROUP_W
SHORT_CONV_W = 3
RG_CONV_W = 4
RG_C = 8.0
GMLP_CHUNK = 128
ATTN_PATTERNS = ((128, 1), (512, 4), (2048, 16))
ATTN_BLOCK = 128
NORM_EPS = 1e-6

kernel_name = "hybrid_parallel_groups_conv_rglru_gmlp_dilated_attn"


def rms_norm(x, g):
    xf = x.astype(jnp.float32)
    y = xf * lax.rsqrt(jnp.mean(xf * xf, axis=-1, keepdims=True) + NORM_EPS)
    return (y * g.astype(jnp.float32)).astype(x.dtype)


def causal_depthwise_conv(x, w):
    K, C = w.shape
    return lax.conv_general_dilated(
        x, w[:, None, :].astype(x.dtype), window_strides=(1,), padding=[(K - 1, 0)],
        dimension_numbers=("NWC", "WIO", "NWC"), feature_group_count=C)


def rg_lru(xb, wa, ba, wx, bx, lam):
    B, S, C = xb.shape
    xh = xb.reshape(B, S, HEADS_PER_MIXER, HEAD_DIM)
    r = jax.nn.sigmoid(jnp.einsum('bshi,hij->bshj', xh, wa).reshape(B, S, C) + ba)
    i = jax.nn.sigmoid(jnp.einsum('bshi,hij->bshj', xh, wx).reshape(B, S, C) + bx)
    log_a = (-RG_C * r.astype(jnp.float32)) * jax.nn.softplus(-lam.astype(jnp.float32))
    a = jnp.exp(log_a)
    mult = jnp.sqrt(-jnp.expm1(2.0 * log_a))
    b = mult * (i * xb).astype(jnp.float32)

    def combine(e1, e2):
        a1, b1 = e1
        a2, b2 = e2
        return a1 * a2, a2 * b1 + b2

    _, h = lax.associative_scan(combine, (a, b), axis=1)
    return h.astype(xb.dtype)


def dilated_window_attention(q, k, v, slopes, window, dil):
    B, S, H, Dh = q.shape
    n_back = window // dil
    span = dil * ATTN_BLOCK
    Sp = -(-S // span) * span
    pad = Sp - S
    if pad:
        pw = ((0, 0), (0, pad), (0, 0), (0, 0))
        q, k, v = jnp.pad(q, pw), jnp.pad(k, pw), jnp.pad(v, pw)
    L = Sp // dil
    nb = L // ATTN_BLOCK
    to_blocks = lambda t: t.reshape(B, nb, ATTN_BLOCK, dil, H, Dh)
    qb, kb, vb = to_blocks(q), to_blocks(k), to_blocks(v)

    def with_prev(t):
        prev = jnp.pad(t[:, :-1], ((0, 0), (1, 0), (0, 0), (0, 0), (0, 0), (0, 0)))
        return jnp.concatenate([prev, t], axis=2)

    kk, vv = with_prev(kb), with_prev(vb)
    scale = 1.0 / math.sqrt(Dh)
    s = jnp.einsum('bnqrhd,bnkrhd->bnrhqk', qb.astype(jnp.float32),
                   kk.astype(jnp.float32)) * scale
    qi = jnp.arange(ATTN_BLOCK)[:, None]
    ki = jnp.arange(2 * ATTN_BLOCK)[None, :]
    delta = qi + ATTN_BLOCK - ki
    band = (delta >= 0) & (delta <= n_back)
    first_ok = (jnp.arange(nb)[:, None, None] > 0) | (ki[None] >= ATTN_BLOCK)
    mask = band[None] & first_ok
    bias = -slopes[:, None, None] * (delta * dil).astype(jnp.float32)[None]
    s = jnp.where(mask[None, :, None, None], s + bias, -jnp.inf)
    m = jnp.max(s, axis=-1, keepdims=True)
    p = jnp.exp(s - m)
    l = jnp.sum(p, axis=-1, keepdims=True)
    o = jnp.einsum('bnrhqk,bnkrhd->bnqrhd', p, vv.astype(jnp.float32))
    o = o / jnp.transpose(l, (0, 1, 4, 2, 3, 5))
    lse = jnp.transpose((m + jnp.log(l))[..., 0], (0, 1, 4, 2, 3))
    o = o.reshape(B, Sp, H, Dh)[:, :S]
    lse = lse.reshape(B, Sp, H)[:, :S]
    return o, lse


def setup_inputs(seed: int = 0) -> dict:
    key = jax.random.key(seed)
    ks = jax.random.split(key, 16)
    f32 = jnp.float32
    nrm = lambda k, shape, sc: jax.random.normal(k, shape, f32) * sc
    x = jax.random.normal(ks[0], (BATCH, SEQ, D_MODEL), f32)
    norm_g = 1.0 + nrm(ks[1], (DEPTH, D_MODEL), 0.01)
    w_in = nrm(ks[2], (DEPTH, D_MODEL, D_IN), D_MODEL ** -0.5)
    conv_a_w = nrm(ks[3], (DEPTH, SHORT_CONV_W, GROUP_W), SHORT_CONV_W ** -0.5)
    conv_r_w = nrm(ks[4], (DEPTH, RG_CONV_W, GROUP_W), RG_CONV_W ** -0.5)
    conv_r_b = nrm(ks[5], (DEPTH, GROUP_W), 0.01)
    lru_wa = nrm(ks[6], (DEPTH, HEADS_PER_MIXER, HEAD_DIM, HEAD_DIM), HEAD_DIM ** -0.5)
    lru_ba = nrm(ks[7], (DEPTH, GROUP_W), 0.01)
    lru_wx = nrm(ks[8], (DEPTH, HEADS_PER_MIXER, HEAD_DIM, HEAD_DIM), HEAD_DIM ** -0.5)
    lru_bx = nrm(ks[9], (DEPTH, GROUP_W), 0.01)
    a_c = jax.random.uniform(ks[10], (DEPTH, GROUP_W), f32, 0.9, 0.999)
    sig = a_c ** (1.0 / RG_C)
    lru_lambda = jnp.log(sig) - jnp.log1p(-sig)
    gmlp_norm_g = 1.0 + nrm(ks[11], (DEPTH, GROUP_W), 0.01)
    gmlp_ws = nrm(ks[12], (DEPTH, HEADS_PER_MIXER, GMLP_CHUNK, GMLP_CHUNK), GMLP_CHUNK ** -0.5)
    gmlp_bs = 1.0 + nrm(ks[13], (DEPTH, HEADS_PER_MIXER, GMLP_CHUNK), 0.1)
    w_out = nrm(ks[14], (DEPTH, D_MIX, D_MODEL), D_MIX ** -0.5)
    final_g = 1.0 + nrm(ks[15], (D_MODEL,), 0.01)
    return {"x": x, "norm_g": norm_g, "w_in": w_in, "conv_a_w": conv_a_w,
            "conv_r_w": conv_r_w, "conv_r_b": conv_r_b, "lru_wa": lru_wa,
            "lru_ba": lru_ba, "lru_wx": lru_wx, "lru_bx": lru_bx,
            "lru_lambda": lru_lambda, "gmlp_norm_g": gmlp_norm_g,
            "gmlp_ws": gmlp_ws, "gmlp_bs": gmlp_bs, "w_out": w_out,
            "final_g": final_g}


def reference(x, norm_g, w_in, conv_a_w, conv_r_w, conv_r_b, lru_wa, lru_ba,
              lru_wx, lru_bx, lru_lambda, gmlp_norm_g, gmlp_ws, gmlp_bs, w_out,
              final_g):
    B, S, _ = x.shape
    H = HEADS_PER_MIXER
    slopes = 2.0 ** (-8.0 * jnp.arange(1, H + 1, dtype=jnp.float32) / H)
    causal_chunk = jnp.tril(jnp.ones((GMLP_CHUNK, GMLP_CHUNK), dtype=bool))
    n_chunk = S // GMLP_CHUNK
    for l in range(DEPTH):
        h = rms_norm(x, norm_g[l])
        z = jnp.einsum('bsd,de->bse', h, w_in[l])
        (a_x, a_b, a_c, a_g,
         r_x, r_g,
         c_u, c_v, c_g,
         d_q, d_k, d_v, d_g) = jnp.split(z, N_CHUNKS, axis=-1)

        y_a = a_b * causal_depthwise_conv(a_c * a_x, conv_a_w[l]) * jax.nn.silu(a_g)

        xb = causal_depthwise_conv(r_x, conv_r_w[l]) + conv_r_b[l]
        y_b = rg_lru(xb, lru_wa[l], lru_ba[l], lru_wx[l], lru_bx[l], lru_lambda[l]) * jax.nn.silu(r_g)

        u = jax.nn.gelu(c_u)
        vv = rms_norm(jax.nn.gelu(c_v), gmlp_norm_g[l])
        vv = vv.reshape(B, n_chunk, GMLP_CHUNK, H, HEAD_DIM)
        ws = jnp.where(causal_chunk[None], gmlp_ws[l], 0.0).astype(vv.dtype)
        sp = jnp.einsum('hts,bnshc->bnthc', ws, vv) + jnp.transpose(gmlp_bs[l])[:, :, None]
        y_c = u * sp.reshape(B, S, GROUP_W) * jax.nn.silu(c_g)

        q = d_q.reshape(B, S, H, HEAD_DIM)
        k = d_k.reshape(B, S, H, HEAD_DIM)
        v = d_v.reshape(B, S, H, HEAD_DIM)
        outs, lses = [], []
        for window, dil in ATTN_PATTERNS:
            o_p, lse_p = dilated_window_attention(q, k, v, slopes, window, dil)
            outs.append(o_p)
            lses.append(lse_p)
        wts = jax.nn.softmax(jnp.stack(lses), axis=0)
        o = jnp.einsum('pbsh,pbshd->bshd', wts, jnp.stack(outs))
        y_d = o.reshape(B, S, GROUP_W).astype(x.dtype) * jax.nn.silu(d_g)

        y = jnp.concatenate([y_a, y_b, y_c, y_d], axis=-1)
        x = x + jnp.einsum('bse,ed->bsd', y, w_out[l])
    return rms_norm(x, final_g)
```

```python
import functools
import math

import jax
import jax.numpy as jnp
from jax import lax
from jax.experimental import pallas as pl
from jax.experimental.pallas import tpu as pltpu

F32 = jnp.float32
BF16 = jnp.bfloat16

D_MODEL = 1024
GROUP_W = 256
HEADS = 4
HEAD_DIM = 64
N_CHUNKS = 13
D_IN = N_CHUNKS * GROUP_W
RG_C = 8.0
GMLP_CHUNK = 128
ATTN_BLOCK = 128
LANES = 128
ATTN_DILATIONS = (1, 4, 16)
ATTN_TILE = ATTN_BLOCK * 16
NORM_EPS = 1e-6
MASKED = -1e30

PROJ_ROWS = 512
MIX_ROWS = 512
CONV_PAD = 8
VMEM_LIMIT = 56 * 1024 * 1024


def _rms(x, g):
    return x * lax.rsqrt(jnp.mean(x * x, axis=-1, keepdims=True) + NORM_EPS) * g


def _head_of_lane(shape, axis):
    return lax.broadcasted_iota(jnp.int32, shape, axis) // HEAD_DIM


def _in_proj_kernel(x_ref, g_ref, w_ref, z_ref):
    h = _rms(x_ref[...], g_ref[...]).astype(BF16)
    for c in range(N_CHUNKS):
        cols = slice(c * GROUP_W, (c + 1) * GROUP_W)
        z_ref[:, cols] = jnp.dot(h, w_ref[:, cols], preferred_element_type=F32)


def _in_proj(x2, g, w_bf16):
    n = x2.shape[0]
    return pl.pallas_call(
        _in_proj_kernel,
        grid=(n // PROJ_ROWS,),
        in_specs=[
            pl.BlockSpec((PROJ_ROWS, D_MODEL), lambda i: (i, 0)),
            pl.BlockSpec((1, D_MODEL), lambda i: (0, 0)),
            pl.BlockSpec((D_MODEL, D_IN), lambda i: (0, 0)),
        ],
        out_specs=pl.BlockSpec((PROJ_ROWS, D_IN), lambda i: (i, 0)),
        out_shape=jax.ShapeDtypeStruct((n, D_IN), F32),
        compiler_params=pltpu.CompilerParams(
            dimension_semantics=("arbitrary",), vmem_limit_bytes=VMEM_LIMIT),
        name="in_proj",
    )(x2, g, w_bf16)


def _out_proj_kernel(x_ref, yabc_ref, yd_ref, w1_ref, w2_ref, o_ref):
    acc = jnp.dot(yabc_ref[...], w1_ref[...], preferred_element_type=F32)
    acc += jnp.dot(yd_ref[...], w2_ref[...], preferred_element_type=F32)
    o_ref[...] = x_ref[...] + acc


def _out_proj(x2, y_abc, y_d, w1, w2):
    n = x2.shape[0]
    return pl.pallas_call(
        _out_proj_kernel,
        grid=(n // PROJ_ROWS,),
        in_specs=[
            pl.BlockSpec((PROJ_ROWS, D_MODEL), lambda i: (i, 0)),
            pl.BlockSpec((PROJ_ROWS, 3 * GROUP_W), lambda i: (i, 0)),
            pl.BlockSpec((PROJ_ROWS, GROUP_W), lambda i: (i, 0)),
            pl.BlockSpec((3 * GROUP_W, D_MODEL), lambda i: (0, 0)),
            pl.BlockSpec((GROUP_W, D_MODEL), lambda i: (0, 0)),
        ],
        out_specs=pl.BlockSpec((PROJ_ROWS, D_MODEL), lambda i: (i, 0)),
        out_shape=jax.ShapeDtypeStruct((n, D_MODEL), F32),
        compiler_params=pltpu.CompilerParams(
            dimension_semantics=("arbitrary",), vmem_limit_bytes=VMEM_LIMIT),
        name="out_proj",
    )(x2, y_abc, y_d, w1, w2)


def _final_norm_kernel(x_ref, g_ref, o_ref):
    o_ref[...] = _rms(x_ref[...], g_ref[...])


def _final_norm(x2, g):
    n = x2.shape[0]
    return pl.pallas_call(
        _final_norm_kernel,
        grid=(n // PROJ_ROWS,),
        in_specs=[
            pl.BlockSpec((PROJ_ROWS, D_MODEL), lambda i: (i, 0)),
            pl.BlockSpec((1, D_MODEL), lambda i: (0, 0)),
        ],
        out_specs=pl.BlockSpec((PROJ_ROWS, D_MODEL), lambda i: (i, 0)),
        out_shape=jax.ShapeDtypeStruct((n, D_MODEL), F32),
        compiler_params=pltpu.CompilerParams(dimension_semantics=("arbitrary",)),
        name="final_norm",
    )(x2, g)


def _causal_conv(buf_ref, carry_ref, x, w_ref):
    rows = x.shape[0]
    taps = w_ref.shape[0]
    buf_ref[0:CONV_PAD, :] = carry_ref[...]
    buf_ref[CONV_PAD:CONV_PAD + rows, :] = x
    carry_ref[...] = x[rows - CONV_PAD:, :]
    out = x * w_ref[taps - 1:taps, :]
    for k in range(taps - 1):
        start = CONV_PAD - (taps - 1) + k
        out += buf_ref[start:start + rows, :] * w_ref[k:k + 1, :]
    return out


def _linear_scan(a, b):
    rows = a.shape[0]
    row = lax.broadcasted_iota(jnp.int32, a.shape, 0)
    d = 1
    while d < rows:
        if d < 8:
            keep = row >= d
            a_prev = jnp.where(keep, pltpu.roll(a, d, axis=0), 1.0)
            b_prev = jnp.where(keep, pltpu.roll(b, d, axis=0), 0.0)
        else:
            a_prev = jnp.concatenate([jnp.ones((d, a.shape[1]), F32), a[:rows - d]], axis=0)
            b_prev = jnp.concatenate([jnp.zeros((d, a.shape[1]), F32), b[:rows - d]], axis=0)
        b = a * b_prev + b
        a = a * a_prev
        d *= 2
    return a, b


def _mixer_kernel(ax_ref, ab_ref, ac_ref, ag_ref, rx_ref, rg_ref, cu_ref, cv_ref, cg_ref,
                  conv_a_w_ref, conv_r_w_ref, conv_r_b_ref, wa_ref, ba_ref, wx_ref, bx_ref,
                  lam_ref, gmlp_g_ref, ws_ref, bs_ref,
                  y_ref,
                  buf_a, buf_r, carry_a, carry_r, carry_h):
    @pl.when(pl.program_id(1) == 0)
    def _():
        carry_a[...] = jnp.zeros_like(carry_a)
        carry_r[...] = jnp.zeros_like(carry_r)
        carry_h[...] = jnp.zeros_like(carry_h)

    rows = ax_ref.shape[0]

    conv_a = _causal_conv(buf_a, carry_a, ac_ref[...] * ax_ref[...], conv_a_w_ref)
    y_ref[:, 0:GROUP_W] = (ab_ref[...] * conv_a * jax.nn.silu(ag_ref[...])).astype(y_ref.dtype)

    xb = _causal_conv(buf_r, carry_r, rx_ref[...], conv_r_w_ref) + conv_r_b_ref[...]
    xb16 = xb.astype(BF16)
    r = jax.nn.sigmoid(jnp.dot(xb16, wa_ref[...], preferred_element_type=F32) + ba_ref[...])
    i = jax.nn.sigmoid(jnp.dot(xb16, wx_ref[...], preferred_element_type=F32) + bx_ref[...])
    log_a = (-RG_C * r) * jax.nn.softplus(-lam_ref[...])
    a = jnp.exp(log_a)
    th = jnp.tanh(log_a)
    mult = jnp.sqrt(-2.0 * th / (1.0 - th))
    a_cum, h = _linear_scan(a, mult * (i * xb))
    h = h + a_cum * carry_h[0:1, :]
    carry_h[...] = jnp.broadcast_to(h[rows - 1:rows, :], carry_h.shape)
    y_ref[:, GROUP_W:2 * GROUP_W] = (h * jax.nn.silu(rg_ref[...])).astype(y_ref.dtype)

    vv = _rms(jax.nn.gelu(cv_ref[...]), gmlp_g_ref[...])
    t_idx = lax.broadcasted_iota(jnp.int32, ws_ref.shape, 0)
    s_idx = lax.broadcasted_iota(jnp.int32, ws_ref.shape, 1) % GMLP_CHUNK
    ws = jnp.where(s_idx <= t_idx, ws_ref[...], 0.0).astype(BF16)
    head = _head_of_lane((GMLP_CHUNK, GROUP_W), 1)
    for j in range(rows // GMLP_CHUNK):
        sl = slice(j * GMLP_CHUNK, (j + 1) * GMLP_CHUNK)
        vj = vv[sl, :]
        v_heads = jnp.concatenate(
            [jnp.where(head == hh, vj, 0.0) for hh in range(HEADS)], axis=0).astype(BF16)
        sp = jnp.dot(ws, v_heads, preferred_element_type=F32) + bs_ref[...]
        yc = jax.nn.gelu(cu_ref[sl, :]) * sp * jax.nn.silu(cg_ref[sl, :])
        y_ref[sl, 2 * GROUP_W:3 * GROUP_W] = yc.astype(y_ref.dtype)


def _mixers(z, batch, seq, p):
    tiles = seq // MIX_ROWS

    def zspec(c):
        return pl.BlockSpec((MIX_ROWS, GROUP_W), lambda b, t, c=c: (b * tiles + t, c))

    def full(a):
        return pl.BlockSpec(a.shape, lambda b, t: (0,) * a.ndim)

    params = [p["conv_a_w"], p["conv_r_w"], p["conv_r_b"], p["wa"], p["ba"], p["wx"], p["bx"],
              p["lam"], p["gmlp_g"], p["ws"], p["bs"]]
    return pl.pallas_call(
        _mixer_kernel,
        grid=(batch, tiles),
        in_specs=[zspec(c) for c in range(9)] + [full(a) for a in params],
        out_specs=pl.BlockSpec((MIX_ROWS, 3 * GROUP_W), lambda b, t: (b * tiles + t, 0)),
        out_shape=jax.ShapeDtypeStruct((batch * seq, 3 * GROUP_W), BF16),
        scratch_shapes=[
            pltpu.VMEM((MIX_ROWS + CONV_PAD, GROUP_W), F32),
            pltpu.VMEM((MIX_ROWS + CONV_PAD, GROUP_W), F32),
            pltpu.VMEM((CONV_PAD, GROUP_W), F32),
            pltpu.VMEM((CONV_PAD, GROUP_W), F32),
            pltpu.VMEM((8, GROUP_W), F32),
        ],
        compiler_params=pltpu.CompilerParams(
            dimension_semantics=("arbitrary", "arbitrary"), vmem_limit_bytes=VMEM_LIMIT),
        name="mixers_abc",
    )(*([z] * 9), *params)


def _attn_kernel(q0, q1, kp0, kp1, kc0, kc1, vp0, vp1, vc0, vc1, g_ref, y_ref,
                 bias_ref, acc_ref, m_ref, l_ref):
    q_ref, kp_ref, kc_ref, vp_ref, vc_ref = (q0, q1), (kp0, kp1), (kc0, kc1), (vp0, vp1), (vc0, vc1)
    n = pl.program_id(1)
    first_tile = (n == 0).astype(jnp.int32)
    scale = 1.0 / math.sqrt(HEAD_DIM)

    def ld(halves, rows):
        return jnp.concatenate([h[rows, :] for h in halves], axis=1)

    def ld_state(ref, rows):
        return jnp.concatenate([ref[0, rows, :], ref[1, rows, :]], axis=1)

    def st_state(ref, rows, val):
        ref[0, rows, :] = val[:, :LANES]
        ref[1, rows, :] = val[:, LANES:]

    qi = lax.broadcasted_iota(jnp.int32, (ATTN_BLOCK, 2 * ATTN_BLOCK), 0)
    kj = lax.broadcasted_iota(jnp.int32, (ATTN_BLOCK, 2 * ATTN_BLOCK), 1)
    delta = qi + ATTN_BLOCK - kj
    band = (delta >= 0) & (delta <= ATTN_BLOCK)
    for p, dil in enumerate(ATTN_DILATIONS):
        for hh in range(HEADS):
            slope = 2.0 ** (-8.0 * (hh + 1) / HEADS)
            bias = -slope * (delta * dil).astype(F32)
            bias_ref[(p * HEADS + hh) * 2] = jnp.where(band, bias, MASKED)
            bias_ref[(p * HEADS + hh) * 2 + 1] = jnp.where(
                band & (kj >= ATTN_BLOCK), bias, MASKED)

    head_q = _head_of_lane((ATTN_BLOCK, GROUP_W), 1)

    def expand(cols):
        out = cols[HEADS - 1]
        for hh in range(HEADS - 2, -1, -1):
            out = jnp.where(head_q == hh, cols[hh], out)
        return out

    def block(p, first, rows, k_prev, k_cur, v_prev, v_cur, mode):
        q = ld(q_ref, rows) * scale
        kk = jnp.concatenate([k_prev, k_cur], axis=0).astype(BF16)
        vv = jnp.concatenate([v_prev, v_cur], axis=0).astype(BF16)
        q_heads = jnp.concatenate(
            [jnp.where(head_q == hh, q, 0.0) for hh in range(HEADS)], axis=0).astype(BF16)
        s_all = lax.dot_general(q_heads, kk, (((1,), (1,)), ((), ())),
                                preferred_element_type=F32)
        probs, m_cols, l_cols = [], [], []
        for hh in range(HEADS):
            s = s_all[hh * ATTN_BLOCK:(hh + 1) * ATTN_BLOCK, :]
            s = s + bias_ref[(p * HEADS + hh) * 2 + first]
            m_h = jnp.max(s, axis=-1, keepdims=True)
            e = jnp.exp(s - m_h)
            probs.append(e.astype(BF16))
            m_cols.append(m_h)
            l_cols.append(jnp.sum(e, axis=-1, keepdims=True))
        o_all = jnp.dot(jnp.concatenate(probs, axis=0), vv,
                        preferred_element_type=F32)
        acc = expand([o_all[hh * ATTN_BLOCK:(hh + 1) * ATTN_BLOCK, :] for hh in range(HEADS)])
        m_b = expand(m_cols)
        l_b = expand(l_cols)
        if mode != "init":
            m_s, l_s, acc_s = ld_state(m_ref, rows), ld_state(l_ref, rows), ld_state(acc_ref, rows)
            m_n = jnp.maximum(m_s, m_b)
            w_s = jnp.exp(m_s - m_n)
            w_b = jnp.exp(m_b - m_n)
            acc = acc_s * w_s + acc * w_b
            l_b = l_s * w_s + l_b * w_b
            m_b = m_n
        if mode == "final":
            y_ref[rows, :] = (acc / l_b * jax.nn.silu(g_ref[rows, :])).astype(y_ref.dtype)
        else:
            st_state(m_ref, rows, m_b)
            st_state(l_ref, rows, l_b)
            st_state(acc_ref, rows, acc)

    def body16(r, carry):
        rows = pl.ds(r, ATTN_BLOCK, stride=16)
        block(2, first_tile, rows, ld(kp_ref, rows), ld(kc_ref, rows),
              ld(vp_ref, rows), ld(vc_ref, rows), "init")
        return carry
    lax.fori_loop(0, 16, body16, 0)

    span = 4 * ATTN_BLOCK
    for nb in range(ATTN_TILE // span):
        def body4(r, carry, nb=nb):
            rows = pl.ds(nb * span + r, ATTN_BLOCK, stride=4)
            if nb == 0:
                prev = pl.ds(ATTN_TILE - span + r, ATTN_BLOCK, stride=4)
                block(1, first_tile, rows, ld(kp_ref, prev), ld(kc_ref, rows),
                      ld(vp_ref, prev), ld(vc_ref, rows), "merge")
            else:
                prev = pl.ds((nb - 1) * span + r, ATTN_BLOCK, stride=4)
                block(1, 0, rows, ld(kc_ref, prev), ld(kc_ref, rows),
                      ld(vc_ref, prev), ld(vc_ref, rows), "merge")
            return carry
        lax.fori_loop(0, 4, body4, 0)

    rows0 = pl.ds(0, ATTN_BLOCK)
    prev0 = pl.ds(ATTN_TILE - ATTN_BLOCK, ATTN_BLOCK)
    block(0, first_tile, rows0, ld(kp_ref, prev0), ld(kc_ref, rows0),
          ld(vp_ref, prev0), ld(vc_ref, rows0), "final")

    def body1(nb, carry):
        start = pl.multiple_of(nb * ATTN_BLOCK, ATTN_BLOCK)
        rows = pl.ds(start, ATTN_BLOCK)
        prev = pl.ds(start - ATTN_BLOCK, ATTN_BLOCK)
        block(0, 0, rows, ld(kc_ref, prev), ld(kc_ref, rows),
              ld(vc_ref, prev), ld(vc_ref, rows), "final")
        return carry
    lax.fori_loop(1, ATTN_TILE // ATTN_BLOCK, body1, 0)


def _attention(z, batch, seq):
    tiles = seq // ATTN_TILE
    halves = GROUP_W // LANES

    def cur(c):
        return [pl.BlockSpec((ATTN_TILE, LANES), lambda b, n, c=c, h=h: (b * tiles + n, c * halves + h))
                for h in range(halves)]

    def prev(c):
        return [pl.BlockSpec((ATTN_TILE, LANES),
                             lambda b, n, c=c, h=h: (b * tiles + jnp.maximum(n - 1, 0), c * halves + h))
                for h in range(halves)]

    in_specs = cur(9) + prev(10) + cur(10) + prev(11) + cur(11) + [
        pl.BlockSpec((ATTN_TILE, GROUP_W), lambda b, n: (b * tiles + n, 12))]
    state = pltpu.VMEM((halves, ATTN_TILE, LANES), F32)
    return pl.pallas_call(
        _attn_kernel,
        grid=(batch, tiles),
        in_specs=in_specs,
        out_specs=pl.BlockSpec((ATTN_TILE, GROUP_W), lambda b, n: (b * tiles + n, 0)),
        out_shape=jax.ShapeDtypeStruct((batch * seq, GROUP_W), BF16),
        scratch_shapes=[
            pltpu.VMEM((len(ATTN_DILATIONS) * HEADS * 2, ATTN_BLOCK, 2 * ATTN_BLOCK), F32),
            state, state, state,
        ],
        compiler_params=pltpu.CompilerParams(
            dimension_semantics=("arbitrary", "arbitrary"), vmem_limit_bytes=VMEM_LIMIT),
        name="dilated_attention",
    )(*([z] * len(in_specs)))


def _block_diag(w):
    h, d, _ = w.shape
    eye = jnp.eye(h, dtype=w.dtype)
    return jnp.einsum("hij,hg->higj", w, eye).reshape(h * d, h * d)


def kernel(x, norm_g, w_in, conv_a_w, conv_r_w, conv_r_b, lru_wa, lru_ba, lru_wx, lru_bx,
           lru_lambda, gmlp_norm_g, gmlp_ws, gmlp_bs, w_out, final_g):
    batch, seq, _ = x.shape
    depth = w_in.shape[0]
    x2 = x.reshape(batch * seq, D_MODEL)
    row = lambda v: v.reshape(1, -1)
    for l in range(depth):
        z = _in_proj(x2, row(norm_g[l]), w_in[l].astype(BF16))
        p = {
            "conv_a_w": conv_a_w[l], "conv_r_w": conv_r_w[l], "conv_r_b": row(conv_r_b[l]),
            "wa": _block_diag(lru_wa[l]).astype(BF16), "ba": row(lru_ba[l]),
            "wx": _block_diag(lru_wx[l]).astype(BF16), "bx": row(lru_bx[l]),
            "lam": row(lru_lambda[l]), "gmlp_g": row(gmlp_norm_g[l]),
            "ws": jnp.transpose(gmlp_ws[l], (1, 0, 2)).reshape(GMLP_CHUNK, HEADS * GMLP_CHUNK),
            "bs": jnp.repeat(jnp.transpose(gmlp_bs[l]), HEAD_DIM, axis=1),
        }
        y_abc = _mixers(z, batch, seq, p)
        y_d = _attention(z, batch, seq)
        w_o = w_out[l].astype(BF16)
        x2 = _out_proj(x2, y_abc, y_d, w_o[:3 * GROUP_W], w_o[3 * GROUP_W:])
    return _final_norm(x2, row(final_g)).reshape(batch, seq, D_MODEL)
```

```python
import functools
import math

import jax
import jax.numpy as jnp
from jax import lax
from jax.experimental import pallas as pl
from jax.experimental.pallas import tpu as pltpu

F32 = jnp.float32
BF16 = jnp.bfloat16

D_MODEL = 1024
GROUP_W = 256
HEADS = 4
HEAD_DIM = 64
N_CHUNKS = 13
D_IN = N_CHUNKS * GROUP_W
RG_C = 8.0
GMLP_CHUNK = 128
ATTN_BLOCK = 128
LANES = 128
ATTN_DILATIONS = (1, 4, 16)
ATTN_TILE = ATTN_BLOCK * 16
NORM_EPS = 1e-6
MASKED = -1e30

PROJ_ROWS = 512
MIX_ROWS = 512
SUBLANES = 8
CONV_PAD = SUBLANES
VMEM_LIMIT = 56 * 1024 * 1024


def _rms(x, g):
    return x * lax.rsqrt(jnp.mean(x * x, axis=-1, keepdims=True) + NORM_EPS) * g


def _head_of_lane(shape, axis):
    return lax.broadcasted_iota(jnp.int32, shape, axis) // HEAD_DIM


def _in_proj_kernel(x_ref, g_ref, w_ref, z_ref):
    h = _rms(x_ref[...], g_ref[...]).astype(BF16)
    for c in range(N_CHUNKS):
        cols = slice(c * GROUP_W, (c + 1) * GROUP_W)
        z_ref[:, cols] = jnp.dot(h, w_ref[:, cols], preferred_element_type=F32)


def _in_proj(x2, g, w_bf16):
    n = x2.shape[0]
    return pl.pallas_call(
        _in_proj_kernel,
        grid=(n // PROJ_ROWS,),
        in_specs=[
            pl.BlockSpec((PROJ_ROWS, D_MODEL), lambda i: (i, 0)),
            pl.BlockSpec((1, D_MODEL), lambda i: (0, 0)),
            pl.BlockSpec((D_MODEL, D_IN), lambda i: (0, 0)),
        ],
        out_specs=pl.BlockSpec((PROJ_ROWS, D_IN), lambda i: (i, 0)),
        out_shape=jax.ShapeDtypeStruct((n, D_IN), F32),
        compiler_params=pltpu.CompilerParams(
            dimension_semantics=("arbitrary",), vmem_limit_bytes=VMEM_LIMIT),
        name="in_proj",
    )(x2, g, w_bf16)


def _residual(x_ref, yabc_ref, yd_ref, w1_ref, w2_ref):
    acc = jnp.dot(yabc_ref[...], w1_ref[...], preferred_element_type=F32)
    acc += jnp.dot(yd_ref[...], w2_ref[...], preferred_element_type=F32)
    return x_ref[...] + acc


def _out_in_proj_kernel(x_ref, yabc_ref, yd_ref, w1_ref, w2_ref, g_ref, w_ref, xo_ref, z_ref):
    x_new = _residual(x_ref, yabc_ref, yd_ref, w1_ref, w2_ref)
    xo_ref[...] = x_new
    h = _rms(x_new, g_ref[...]).astype(BF16)
    for c in range(N_CHUNKS):
        cols = slice(c * GROUP_W, (c + 1) * GROUP_W)
        z_ref[:, cols] = jnp.dot(h, w_ref[:, cols], preferred_element_type=F32)


def _out_final_kernel(x_ref, yabc_ref, yd_ref, w1_ref, w2_ref, g_ref, o_ref):
    o_ref[...] = _rms(_residual(x_ref, yabc_ref, yd_ref, w1_ref, w2_ref), g_ref[...])


def _out_proj_specs():
    return [
        pl.BlockSpec((PROJ_ROWS, D_MODEL), lambda i: (i, 0)),
        pl.BlockSpec((PROJ_ROWS, 3 * GROUP_W), lambda i: (i, 0)),
        pl.BlockSpec((PROJ_ROWS, GROUP_W), lambda i: (i, 0)),
        pl.BlockSpec((3 * GROUP_W, D_MODEL), lambda i: (0, 0)),
        pl.BlockSpec((GROUP_W, D_MODEL), lambda i: (0, 0)),
        pl.BlockSpec((1, D_MODEL), lambda i: (0, 0)),
    ]


def _out_in_proj(x2, y_abc, y_d, w1, w2, g, w_bf16):
    n = x2.shape[0]
    return pl.pallas_call(
        _out_in_proj_kernel,
        grid=(n // PROJ_ROWS,),
        in_specs=_out_proj_specs() + [pl.BlockSpec((D_MODEL, D_IN), lambda i: (0, 0))],
        out_specs=[pl.BlockSpec((PROJ_ROWS, D_MODEL), lambda i: (i, 0)),
                   pl.BlockSpec((PROJ_ROWS, D_IN), lambda i: (i, 0))],
        out_shape=[jax.ShapeDtypeStruct((n, D_MODEL), F32),
                   jax.ShapeDtypeStruct((n, D_IN), F32)],
        compiler_params=pltpu.CompilerParams(
            dimension_semantics=("arbitrary",), vmem_limit_bytes=VMEM_LIMIT),
        name="out_in_proj",
    )(x2, y_abc, y_d, w1, w2, g, w_bf16)


def _out_final(x2, y_abc, y_d, w1, w2, g):
    n = x2.shape[0]
    return pl.pallas_call(
        _out_final_kernel,
        grid=(n // PROJ_ROWS,),
        in_specs=_out_proj_specs(),
        out_specs=pl.BlockSpec((PROJ_ROWS, D_MODEL), lambda i: (i, 0)),
        out_shape=jax.ShapeDtypeStruct((n, D_MODEL), F32),
        compiler_params=pltpu.CompilerParams(
            dimension_semantics=("arbitrary",), vmem_limit_bytes=VMEM_LIMIT),
        name="out_final",
    )(x2, y_abc, y_d, w1, w2, g)


def _causal_conv(carry_ref, x, w_ref):
    rows = x.shape[0]
    taps = w_ref.shape[0]
    history = carry_ref[...]
    carry_ref[...] = x[rows - CONV_PAD:, :]
    row = lax.broadcasted_iota(jnp.int32, history.shape, 0)
    out = x * w_ref[taps - 1:taps, :]
    for k in range(taps - 1):
        shift = taps - 1 - k
        rolled = pltpu.roll(x, shift, axis=0)
        head = jnp.where(row < shift, pltpu.roll(history, shift, axis=0), rolled[0:CONV_PAD])
        shifted = jnp.concatenate([head, rolled[CONV_PAD:]], axis=0)
        out += shifted * w_ref[k:k + 1, :]
    return out


def _linear_scan(a, b, h0):
    rows, cols = a.shape
    groups = rows // SUBLANES
    a = a.reshape(groups, SUBLANES, cols)
    b = b.reshape(groups, SUBLANES, cols)
    row = lax.broadcasted_iota(jnp.int32, a.shape, 1)
    d = 1
    while d < SUBLANES:
        keep = row >= d
        a_prev = jnp.where(keep, pltpu.roll(a, d, axis=1), 1.0)
        b_prev = jnp.where(keep, pltpu.roll(b, d, axis=1), 0.0)
        b = a * b_prev + b
        a = a * a_prev
        d *= 2
    out = []
    for g in range(groups):
        hg = b[g] + a[g] * h0
        out.append(hg)
        h0 = hg[SUBLANES - 1:SUBLANES, :]
    return jnp.concatenate(out, axis=0)


def _mixer_kernel(ax_ref, ab_ref, ac_ref, ag_ref, rx_ref, rg_ref, cu_ref, cv_ref, cg_ref,
                  conv_a_w_ref, conv_r_w_ref, conv_r_b_ref, wa_ref, ba_ref, wx_ref, bx_ref,
                  lam_ref, gmlp_g_ref, ws_ref, bs_ref,
                  y_ref,
                  carry_a, carry_r, carry_h):
    @pl.when(pl.program_id(1) == 0)
    def _():
        carry_a[...] = jnp.zeros_like(carry_a)
        carry_r[...] = jnp.zeros_like(carry_r)
        carry_h[...] = jnp.zeros_like(carry_h)

    rows = ax_ref.shape[0]

    conv_a = _causal_conv(carry_a, ac_ref[...] * ax_ref[...], conv_a_w_ref)
    y_ref[:, 0:GROUP_W] = (ab_ref[...] * conv_a * jax.nn.silu(ag_ref[...])).astype(y_ref.dtype)

    xb = _causal_conv(carry_r, rx_ref[...], conv_r_w_ref) + conv_r_b_ref[...]
    xb16 = xb.astype(BF16)
    r = jax.nn.sigmoid(jnp.dot(xb16, wa_ref[...], preferred_element_type=F32) + ba_ref[...])
    i = jax.nn.sigmoid(jnp.dot(xb16, wx_ref[...], preferred_element_type=F32) + bx_ref[...])
    log_a = (-RG_C * r) * jax.nn.softplus(-lam_ref[...])
    a = jnp.exp(log_a)
    th = jnp.tanh(log_a)
    mult = jnp.sqrt(-2.0 * th / (1.0 - th))
    h = _linear_scan(a, mult * (i * xb), carry_h[0:1, :])
    carry_h[...] = jnp.broadcast_to(h[rows - 1:rows, :], carry_h.shape)
    y_ref[:, GROUP_W:2 * GROUP_W] = (h * jax.nn.silu(rg_ref[...])).astype(y_ref.dtype)

    vv = _rms(jax.nn.gelu(cv_ref[...]), gmlp_g_ref[...])
    t_idx = lax.broadcasted_iota(jnp.int32, ws_ref.shape, 0)
    s_idx = lax.broadcasted_iota(jnp.int32, ws_ref.shape, 1) % GMLP_CHUNK
    ws = jnp.where(s_idx <= t_idx, ws_ref[...], 0.0).astype(BF16)
    head = _head_of_lane((GMLP_CHUNK, GROUP_W), 1)
    for j in range(rows // GMLP_CHUNK):
        sl = slice(j * GMLP_CHUNK, (j + 1) * GMLP_CHUNK)
        vj = vv[sl, :]
        v_heads = jnp.concatenate(
            [jnp.where(head == hh, vj, 0.0) for hh in range(HEADS)], axis=0).astype(BF16)
        sp = jnp.dot(ws, v_heads, preferred_element_type=F32) + bs_ref[...]
        yc = jax.nn.gelu(cu_ref[sl, :]) * sp * jax.nn.silu(cg_ref[sl, :])
        y_ref[sl, 2 * GROUP_W:3 * GROUP_W] = yc.astype(y_ref.dtype)


def _mixers(z, batch, seq, p):
    tiles = seq // MIX_ROWS

    def zspec(c):
        return pl.BlockSpec((MIX_ROWS, GROUP_W), lambda b, t, c=c: (b * tiles + t, c))

    def full(a):
        return pl.BlockSpec(a.shape, lambda b, t: (0,) * a.ndim)

    params = [p["conv_a_w"], p["conv_r_w"], p["conv_r_b"], p["wa"], p["ba"], p["wx"], p["bx"],
              p["lam"], p["gmlp_g"], p["ws"], p["bs"]]
    return pl.pallas_call(
        _mixer_kernel,
        grid=(batch, tiles),
        in_specs=[zspec(c) for c in range(9)] + [full(a) for a in params],
        out_specs=pl.BlockSpec((MIX_ROWS, 3 * GROUP_W), lambda b, t: (b * tiles + t, 0)),
        out_shape=jax.ShapeDtypeStruct((batch * seq, 3 * GROUP_W), BF16),
        scratch_shapes=[
            pltpu.VMEM((CONV_PAD, GROUP_W), F32),
            pltpu.VMEM((CONV_PAD, GROUP_W), F32),
            pltpu.VMEM((8, GROUP_W), F32),
        ],
        compiler_params=pltpu.CompilerParams(
            dimension_semantics=("arbitrary", "arbitrary"), vmem_limit_bytes=VMEM_LIMIT),
        name="mixers_abc",
    )(*([z] * 9), *params)


def _attn_kernel(q0, q1, kp0, kp1, kc0, kc1, vp0, vp1, vc0, vc1, g_ref, y_ref,
                 bias_ref, acc_ref, m_ref, l_ref):
    q_ref, kp_ref, kc_ref, vp_ref, vc_ref = (q0, q1), (kp0, kp1), (kc0, kc1), (vp0, vp1), (vc0, vc1)
    n = pl.program_id(1)
    first_tile = (n == 0).astype(jnp.int32)
    scale = 1.0 / math.sqrt(HEAD_DIM)

    def ld(halves, rows):
        return jnp.concatenate([h[rows, :] for h in halves], axis=1)

    def ld_state(ref, rows):
        return jnp.concatenate([ref[0, rows, :], ref[1, rows, :]], axis=1)

    def st_state(ref, rows, val):
        ref[0, rows, :] = val[:, :LANES]
        ref[1, rows, :] = val[:, LANES:]

    qi = lax.broadcasted_iota(jnp.int32, (ATTN_BLOCK, 2 * ATTN_BLOCK), 0)
    kj = lax.broadcasted_iota(jnp.int32, (ATTN_BLOCK, 2 * ATTN_BLOCK), 1)
    delta = qi + ATTN_BLOCK - kj
    band = (delta >= 0) & (delta <= ATTN_BLOCK)
    for p, dil in enumerate(ATTN_DILATIONS):
        for hh in range(HEADS):
            slope = 2.0 ** (-8.0 * (hh + 1) / HEADS)
            bias = -slope * (delta * dil).astype(F32)
            bias_ref[(p * HEADS + hh) * 2] = jnp.where(band, bias, MASKED)
            bias_ref[(p * HEADS + hh) * 2 + 1] = jnp.where(
                band & (kj >= ATTN_BLOCK), bias, MASKED)

    head_q = _head_of_lane((ATTN_BLOCK, GROUP_W), 1)

    def expand(cols):
        out = cols[HEADS - 1]
        for hh in range(HEADS - 2, -1, -1):
            out = jnp.where(head_q == hh, cols[hh], out)
        return out

    def run_blocks(mode, specs):
        scores = []
        for p, first, rows, k_prev, k_cur, v_prev, v_cur in specs:
            q = ld(q_ref, rows) * scale
            kk = jnp.concatenate([k_prev, k_cur], axis=0).astype(BF16)
            q_heads = jnp.concatenate(
                [jnp.where(head_q == hh, q, 0.0) for hh in range(HEADS)], axis=0).astype(BF16)
            scores.append(lax.dot_general(q_heads, kk, (((1,), (1,)), ((), ())),
                                          preferred_element_type=F32))
        soft = []
        for (p, first, *_), s_all in zip(specs, scores):
            probs, m_cols, l_cols = [], [], []
            for hh in range(HEADS):
                s = s_all[hh * ATTN_BLOCK:(hh + 1) * ATTN_BLOCK, :]
                s = s + bias_ref[(p * HEADS + hh) * 2 + first]
                m_h = jnp.max(s, axis=-1, keepdims=True)
                e = jnp.exp(s - m_h)
                probs.append(e.astype(BF16))
                m_cols.append(m_h)
                l_cols.append(jnp.sum(e, axis=-1, keepdims=True))
            soft.append((jnp.concatenate(probs, axis=0), m_cols, l_cols))
        for (p, first, rows, k_prev, k_cur, v_prev, v_cur), (p_all, m_cols, l_cols) in zip(specs, soft):
            vv = jnp.concatenate([v_prev, v_cur], axis=0).astype(BF16)
            o_all = jnp.dot(p_all, vv, preferred_element_type=F32)
            acc = expand([o_all[hh * ATTN_BLOCK:(hh + 1) * ATTN_BLOCK, :] for hh in range(HEADS)])
            m_b = expand(m_cols)
            l_b = expand(l_cols)
            if mode != "init":
                m_s, l_s, acc_s = ld_state(m_ref, rows), ld_state(l_ref, rows), ld_state(acc_ref, rows)
                m_n = jnp.maximum(m_s, m_b)
                w_s = jnp.exp(m_s - m_n)
                w_b = jnp.exp(m_b - m_n)
                acc = acc_s * w_s + acc * w_b
                l_b = l_s * w_s + l_b * w_b
                m_b = m_n
            if mode == "final":
                y_ref[rows, :] = (acc / l_b * jax.nn.silu(g_ref[rows, :])).astype(y_ref.dtype)
            else:
                st_state(m_ref, rows, m_b)
                st_state(l_ref, rows, l_b)
                st_state(acc_ref, rows, acc)


    def spec16(r):
        rows = pl.ds(r, ATTN_BLOCK, stride=16)
        return (2, first_tile, rows, ld(kp_ref, rows), ld(kc_ref, rows),
                ld(vp_ref, rows), ld(vc_ref, rows))

    def body16(i, carry):
        run_blocks("init", [spec16(2 * i), spec16(2 * i + 1)])
        return carry
    lax.fori_loop(0, 8, body16, 0)

    span = 4 * ATTN_BLOCK

    def spec4(nb, r):
        rows = pl.ds(nb * span + r, ATTN_BLOCK, stride=4)
        if nb == 0:
            prev = pl.ds(ATTN_TILE - span + r, ATTN_BLOCK, stride=4)
            return (1, first_tile, rows, ld(kp_ref, prev), ld(kc_ref, rows),
                    ld(vp_ref, prev), ld(vc_ref, rows))
        prev = pl.ds((nb - 1) * span + r, ATTN_BLOCK, stride=4)
        return (1, 0, rows, ld(kc_ref, prev), ld(kc_ref, rows),
                ld(vc_ref, prev), ld(vc_ref, rows))

    for nb in range(ATTN_TILE // span):
        def body4(i, carry, nb=nb):
            run_blocks("merge", [spec4(nb, 2 * i), spec4(nb, 2 * i + 1)])
            return carry
        lax.fori_loop(0, 2, body4, 0)

    def spec1(nb):
        start = pl.multiple_of(nb * ATTN_BLOCK, ATTN_BLOCK)
        rows = pl.ds(start, ATTN_BLOCK)
        prev = pl.ds(start - ATTN_BLOCK, ATTN_BLOCK)
        return (0, 0, rows, ld(kc_ref, prev), ld(kc_ref, rows),
                ld(vc_ref, prev), ld(vc_ref, rows))

    rows0 = pl.ds(0, ATTN_BLOCK)
    prev0 = pl.ds(ATTN_TILE - ATTN_BLOCK, ATTN_BLOCK)
    run_blocks("final", [(0, first_tile, rows0, ld(kp_ref, prev0), ld(kc_ref, rows0),
                          ld(vp_ref, prev0), ld(vc_ref, rows0)), spec1(1)])

    def body1(i, carry):
        run_blocks("final", [spec1(2 * i), spec1(2 * i + 1)])
        return carry
    lax.fori_loop(1, ATTN_TILE // ATTN_BLOCK // 2, body1, 0)


def _attention(z, batch, seq):
    tiles = seq // ATTN_TILE
    halves = GROUP_W // LANES

    def cur(c):
        return [pl.BlockSpec((ATTN_TILE, LANES), lambda b, n, c=c, h=h: (b * tiles + n, c * halves + h))
                for h in range(halves)]

    def prev(c):
        return [pl.BlockSpec((ATTN_TILE, LANES),
                             lambda b, n, c=c, h=h: (b * tiles + jnp.maximum(n - 1, 0), c * halves + h))
                for h in range(halves)]

    in_specs = cur(9) + prev(10) + cur(10) + prev(11) + cur(11) + [
        pl.BlockSpec((ATTN_TILE, GROUP_W), lambda b, n: (b * tiles + n, 12))]
    state = pltpu.VMEM((halves, ATTN_TILE, LANES), F32)
    return pl.pallas_call(
        _attn_kernel,
        grid=(batch, tiles),
        in_specs=in_specs,
        out_specs=pl.BlockSpec((ATTN_TILE, GROUP_W), lambda b, n: (b * tiles + n, 0)),
        out_shape=jax.ShapeDtypeStruct((batch * seq, GROUP_W), BF16),
        scratch_shapes=[
            pltpu.VMEM((len(ATTN_DILATIONS) * HEADS * 2, ATTN_BLOCK, 2 * ATTN_BLOCK), F32),
            state, state, state,
        ],
        compiler_params=pltpu.CompilerParams(
            dimension_semantics=("arbitrary", "arbitrary"), vmem_limit_bytes=VMEM_LIMIT),
        name="dilated_attention",
    )(*([z] * len(in_specs)))


def _block_diag(w):
    h, d, _ = w.shape
    eye = jnp.eye(h, dtype=w.dtype)
    return jnp.einsum("hij,hg->higj", w, eye).reshape(h * d, h * d)


def kernel(x, norm_g, w_in, conv_a_w, conv_r_w, conv_r_b, lru_wa, lru_ba, lru_wx, lru_bx,
           lru_lambda, gmlp_norm_g, gmlp_ws, gmlp_bs, w_out, final_g):
    batch, seq, _ = x.shape
    depth = w_in.shape[0]
    x2 = x.reshape(batch * seq, D_MODEL)
    row = lambda v: v.reshape(1, -1)
    w_in16 = w_in.astype(BF16)
    w_out16 = w_out.astype(BF16)
    z = _in_proj(x2, row(norm_g[0]), w_in16[0])
    for l in range(depth):
        p = {
            "conv_a_w": conv_a_w[l], "conv_r_w": conv_r_w[l], "conv_r_b": row(conv_r_b[l]),
            "wa": _block_diag(lru_wa[l]).astype(BF16), "ba": row(lru_ba[l]),
            "wx": _block_diag(lru_wx[l]).astype(BF16), "bx": row(lru_bx[l]),
            "lam": row(lru_lambda[l]), "gmlp_g": row(gmlp_norm_g[l]),
            "ws": jnp.transpose(gmlp_ws[l], (1, 0, 2)).reshape(GMLP_CHUNK, HEADS * GMLP_CHUNK),
            "bs": jnp.repeat(jnp.transpose(gmlp_bs[l]), HEAD_DIM, axis=1),
        }
        y_abc = _mixers(z, batch, seq, p)
        y_d = _attention(z, batch, seq)
        w1, w2 = w_out16[l, :3 * GROUP_W], w_out16[l, 3 * GROUP_W:]
        if l + 1 < depth:
            x2, z = _out_in_proj(x2, y_abc, y_d, w1, w2, row(norm_g[l + 1]), w_in16[l + 1])
        else:
            out = _out_final(x2, y_abc, y_d, w1, w2, row(final_g))
    return out.reshape(batch, seq, D_MODEL)
```

```python
import functools
import math

import jax
import jax.numpy as jnp
from jax import lax
from jax.experimental import pallas as pl
from jax.experimental.pallas import tpu as pltpu

F32 = jnp.float32
BF16 = jnp.bfloat16

D_MODEL = 1024
GROUP_W = 256
HEADS = 4
HEAD_DIM = 64
N_CHUNKS = 13
N_MIX_CHUNKS = 9
D_IN = N_CHUNKS * GROUP_W
D_MIX_ABC = 3 * GROUP_W
D_ATTN = (N_CHUNKS - N_MIX_CHUNKS) * GROUP_W
RG_C = 8.0
GMLP_CHUNK = 128
ATTN_BLOCK = 128
LANES = 128
ATTN_DILATIONS = (1, 4, 16)
ATTN_TILE = ATTN_BLOCK * 16
NORM_EPS = 1e-6
MASKED = -1e30
ATTN_SCALE = 1.0 / math.sqrt(HEAD_DIM)
assert math.frexp(ATTN_SCALE)[0] == 0.5, "softmax scale must be a power of two to fold into q"

LAYER_ROWS = 512
SUBLANES = 8
CONV_PAD = SUBLANES
VMEM_LIMIT = 56 * 1024 * 1024


def _rms(x, g):
    return x * lax.rsqrt(jnp.mean(x * x, axis=-1, keepdims=True) + NORM_EPS) * g


def _head_of_lane(shape, axis):
    return lax.broadcasted_iota(jnp.int32, shape, axis) // HEAD_DIM


def _causal_conv(carry_ref, x, w_ref):
    rows = x.shape[0]
    taps = w_ref.shape[0]
    history = carry_ref[...]
    carry_ref[...] = x[rows - CONV_PAD:, :]
    row = lax.broadcasted_iota(jnp.int32, history.shape, 0)
    out = x * w_ref[taps - 1:taps, :]
    for k in range(taps - 1):
        shift = taps - 1 - k
        rolled = pltpu.roll(x, shift, axis=0)
        head = jnp.where(row < shift, pltpu.roll(history, shift, axis=0), rolled[0:CONV_PAD])
        shifted = jnp.concatenate([head, rolled[CONV_PAD:]], axis=0)
        out += shifted * w_ref[k:k + 1, :]
    return out


def _linear_scan(a, b, h0):
    rows, cols = a.shape
    groups = rows // SUBLANES
    a = a.reshape(groups, SUBLANES, cols)
    b = b.reshape(groups, SUBLANES, cols)
    row = lax.broadcasted_iota(jnp.int32, a.shape, 1)
    d = 1
    while d < SUBLANES:
        keep = row >= d
        a_prev = jnp.where(keep, pltpu.roll(a, d, axis=1), 1.0)
        b_prev = jnp.where(keep, pltpu.roll(b, d, axis=1), 0.0)
        b = a * b_prev + b
        a = a * a_prev
        d *= 2
    out = []
    for g in range(groups):
        hg = b[g] + a[g] * h0
        out.append(hg)
        h0 = hg[SUBLANES - 1:SUBLANES, :]
    return jnp.concatenate(out, axis=0)


def _mixers_abc(z, prm, y_ref, carry_a, carry_r, carry_h):
    (conv_a_w_ref, conv_r_w_ref, conv_r_b_ref, wa_ref, ba_ref, wx_ref, bx_ref,
     lam_ref, gmlp_g_ref, ws_ref, bs_ref) = prm
    a_x, a_b, a_c, a_g, r_x, r_g, c_u, c_v, c_g = (z(c) for c in range(N_MIX_CHUNKS))
    rows = a_x.shape[0]

    conv_a = _causal_conv(carry_a, a_c * a_x, conv_a_w_ref)
    y_ref[:, 0:GROUP_W] = (a_b * conv_a * jax.nn.silu(a_g)).astype(y_ref.dtype)

    xb = _causal_conv(carry_r, r_x, conv_r_w_ref) + conv_r_b_ref[...]
    xb16 = xb.astype(BF16)
    r = jax.nn.sigmoid(jnp.dot(xb16, wa_ref[...], preferred_element_type=F32) + ba_ref[...])
    i = jax.nn.sigmoid(jnp.dot(xb16, wx_ref[...], preferred_element_type=F32) + bx_ref[...])
    log_a = (-RG_C * r) * jax.nn.softplus(-lam_ref[...])
    a = jnp.exp(log_a)
    th = jnp.tanh(log_a)
    mult = jnp.sqrt(-2.0 * th / (1.0 - th))
    h = _linear_scan(a, mult * (i * xb), carry_h[0:1, :])
    carry_h[...] = jnp.broadcast_to(h[rows - 1:rows, :], carry_h.shape)
    y_ref[:, GROUP_W:2 * GROUP_W] = (h * jax.nn.silu(r_g)).astype(y_ref.dtype)

    vv = _rms(jax.nn.gelu(c_v), gmlp_g_ref[...])
    t_idx = lax.broadcasted_iota(jnp.int32, ws_ref.shape, 0)
    s_idx = lax.broadcasted_iota(jnp.int32, ws_ref.shape, 1) % GMLP_CHUNK
    ws = jnp.where(s_idx <= t_idx, ws_ref[...], 0.0).astype(BF16)
    head = _head_of_lane((GMLP_CHUNK, GROUP_W), 1)
    for j in range(rows // GMLP_CHUNK):
        sl = slice(j * GMLP_CHUNK, (j + 1) * GMLP_CHUNK)
        vj = vv[sl, :]
        v_heads = jnp.concatenate(
            [jnp.where(head == hh, vj, 0.0) for hh in range(HEADS)], axis=0).astype(BF16)
        sp = jnp.dot(ws, v_heads, preferred_element_type=F32) + bs_ref[...]
        yc = jax.nn.gelu(c_u[sl, :]) * sp * jax.nn.silu(c_g[sl, :])
        y_ref[sl, 2 * GROUP_W:3 * GROUP_W] = yc.astype(y_ref.dtype)


N_MIX_PARAMS = 11


def _residual(x_ref, yabc_ref, yd_ref, w1_ref, w2_ref):
    acc = jnp.dot(yabc_ref[...], w1_ref[...], preferred_element_type=F32)
    acc += jnp.dot(yd_ref[...], w2_ref[...], preferred_element_type=F32)
    return x_ref[...] + acc


def _layer_kernel(has_residual, *refs):
    refs = list(refs)
    take = lambda n: [refs.pop(0) for _ in range(n)]
    if has_residual:
        x_ref, yabc_ref, yd_ref, w1_ref, w2_ref = take(5)
    else:
        (x_ref,) = take(1)
    g_ref, w_ref = take(2)
    prm = take(N_MIX_PARAMS)
    if has_residual:
        (xo_ref,) = take(1)
    zq_ref, y_ref, carry_a, carry_r, carry_h = refs

    @pl.when(pl.program_id(1) == 0)
    def _():
        carry_a[...] = jnp.zeros_like(carry_a)
        carry_r[...] = jnp.zeros_like(carry_r)
        carry_h[...] = jnp.zeros_like(carry_h)

    if has_residual:
        x = _residual(x_ref, yabc_ref, yd_ref, w1_ref, w2_ref)
        xo_ref[...] = x
    else:
        x = x_ref[...]
    h = _rms(x, g_ref[...]).astype(BF16)

    def z(c):
        cols = slice(c * GROUP_W, (c + 1) * GROUP_W)
        return jnp.dot(h, w_ref[:, cols], preferred_element_type=F32)

    _mixers_abc(z, prm, y_ref, carry_a, carry_r, carry_h)
    zq_ref[:, 0:GROUP_W] = z(N_MIX_CHUNKS) * ATTN_SCALE
    for c in range(N_MIX_CHUNKS + 1, N_CHUNKS):
        cols = slice((c - N_MIX_CHUNKS) * GROUP_W, (c - N_MIX_CHUNKS + 1) * GROUP_W)
        zq_ref[:, cols] = z(c)


def _layer(batch, seq, x2, prev, g, w_in16, prm):
    tiles = seq // LAYER_ROWS
    n = batch * seq
    has_residual = prev is not None

    def rows(width):
        return pl.BlockSpec((LAYER_ROWS, width), lambda b, t: (b * tiles + t, 0))

    def full(a):
        return pl.BlockSpec(a.shape, lambda b, t: (0,) * a.ndim)

    args = [x2] + (list(prev) if has_residual else []) + [g, w_in16] + list(prm)
    in_specs = [rows(D_MODEL)]
    if has_residual:
        in_specs += [rows(D_MIX_ABC), rows(GROUP_W), full(prev[2]), full(prev[3])]
    in_specs += [full(g), full(w_in16)] + [full(a) for a in prm]
    out_specs = [rows(D_ATTN), rows(D_MIX_ABC)]
    out_shape = [jax.ShapeDtypeStruct((n, D_ATTN), F32),
                 jax.ShapeDtypeStruct((n, D_MIX_ABC), BF16)]
    if has_residual:
        out_specs = [rows(D_MODEL)] + out_specs
        out_shape = [jax.ShapeDtypeStruct((n, D_MODEL), F32)] + out_shape
    return pl.pallas_call(
        functools.partial(_layer_kernel, has_residual),
        grid=(batch, tiles),
        in_specs=in_specs,
        out_specs=out_specs,
        out_shape=out_shape,
        scratch_shapes=[
            pltpu.VMEM((CONV_PAD, GROUP_W), F32),
            pltpu.VMEM((CONV_PAD, GROUP_W), F32),
            pltpu.VMEM((SUBLANES, GROUP_W), F32),
        ],
        compiler_params=pltpu.CompilerParams(
            dimension_semantics=("arbitrary", "arbitrary"), vmem_limit_bytes=VMEM_LIMIT),
        name="layer_res" if has_residual else "layer_first",
    )(*args)


def _out_final_kernel(x_ref, yabc_ref, yd_ref, w1_ref, w2_ref, g_ref, o_ref):
    o_ref[...] = _rms(_residual(x_ref, yabc_ref, yd_ref, w1_ref, w2_ref), g_ref[...])


def _out_final(x2, y_abc, y_d, w1, w2, g):
    n = x2.shape[0]
    rows = lambda width: pl.BlockSpec((LAYER_ROWS, width), lambda i: (i, 0))
    full = lambda a: pl.BlockSpec(a.shape, lambda i: (0,) * a.ndim)
    return pl.pallas_call(
        _out_final_kernel,
        grid=(n // LAYER_ROWS,),
        in_specs=[rows(D_MODEL), rows(D_MIX_ABC), rows(GROUP_W), full(w1), full(w2), full(g)],
        out_specs=rows(D_MODEL),
        out_shape=jax.ShapeDtypeStruct((n, D_MODEL), F32),
        compiler_params=pltpu.CompilerParams(
            dimension_semantics=("arbitrary",), vmem_limit_bytes=VMEM_LIMIT),
        name="out_final",
    )(x2, y_abc, y_d, w1, w2, g)


def _attn_kernel(q0, q1, kp0, kp1, kc0, kc1, vp0, vp1, vc0, vc1, g_ref, y_ref,
                 bias_ref, acc_ref, m_ref, l_ref):
    q_ref, kp_ref, kc_ref, vp_ref, vc_ref = (q0, q1), (kp0, kp1), (kc0, kc1), (vp0, vp1), (vc0, vc1)
    n = pl.program_id(1)
    first_tile = (n == 0).astype(jnp.int32)

    def ld(halves, rows):
        return jnp.concatenate([h[rows, :] for h in halves], axis=1)

    def ld_state(ref, rows):
        return jnp.concatenate([ref[0, rows, :], ref[1, rows, :]], axis=1)

    def st_state(ref, rows, val):
        ref[0, rows, :] = val[:, :LANES]
        ref[1, rows, :] = val[:, LANES:]

    qi = lax.broadcasted_iota(jnp.int32, (ATTN_BLOCK, 2 * ATTN_BLOCK), 0)
    kj = lax.broadcasted_iota(jnp.int32, (ATTN_BLOCK, 2 * ATTN_BLOCK), 1)
    delta = qi + ATTN_BLOCK - kj
    band = (delta >= 0) & (delta <= ATTN_BLOCK)
    for p, dil in enumerate(ATTN_DILATIONS):
        for hh in range(HEADS):
            slope = 2.0 ** (-8.0 * (hh + 1) / HEADS)
            bias = -slope * (delta * dil).astype(F32)
            bias_ref[(p * HEADS + hh) * 2] = jnp.where(band, bias, MASKED)
            bias_ref[(p * HEADS + hh) * 2 + 1] = jnp.where(
                band & (kj >= ATTN_BLOCK), bias, MASKED)

    head_q = _head_of_lane((ATTN_BLOCK, GROUP_W), 1)

    def expand(cols):
        out = cols[HEADS - 1]
        for hh in range(HEADS - 2, -1, -1):
            out = jnp.where(head_q == hh, cols[hh], out)
        return out

    def run_blocks(mode, specs):
        scores = []
        for p, first, rows, k_prev, k_cur, v_prev, v_cur in specs:
            q = ld(q_ref, rows)
            kk = jnp.concatenate([k_prev, k_cur], axis=0).astype(BF16)
            q_heads = jnp.concatenate(
                [jnp.where(head_q == hh, q, 0.0) for hh in range(HEADS)], axis=0).astype(BF16)
            scores.append(lax.dot_general(q_heads, kk, (((1,), (1,)), ((), ())),
                                          preferred_element_type=F32))
        soft = []
        for (p, first, *_), s_all in zip(specs, scores):
            probs, m_cols, l_cols = [], [], []
            for hh in range(HEADS):
                s = s_all[hh * ATTN_BLOCK:(hh + 1) * ATTN_BLOCK, :]
                s = s + bias_ref[(p * HEADS + hh) * 2 + first]
                m_h = jnp.max(s, axis=-1, keepdims=True)
                e = jnp.exp(s - m_h)
                probs.append(e.astype(BF16))
                m_cols.append(m_h)
                l_cols.append(jnp.sum(e, axis=-1, keepdims=True))
            soft.append((jnp.concatenate(probs, axis=0), m_cols, l_cols))
        for (p, first, rows, k_prev, k_cur, v_prev, v_cur), (p_all, m_cols, l_cols) in zip(specs, soft):
            vv = jnp.concatenate([v_prev, v_cur], axis=0).astype(BF16)
            o_all = jnp.dot(p_all, vv, preferred_element_type=F32)
            acc = expand([o_all[hh * ATTN_BLOCK:(hh + 1) * ATTN_BLOCK, :] for hh in range(HEADS)])
            m_b = expand(m_cols)
            l_b = expand(l_cols)
            if mode != "init":
                m_s, l_s, acc_s = ld_state(m_ref, rows), ld_state(l_ref, rows), ld_state(acc_ref, rows)
                m_n = jnp.maximum(m_s, m_b)
                w_s = jnp.exp(m_s - m_n)
                w_b = jnp.exp(m_b - m_n)
                acc = acc_s * w_s + acc * w_b
                l_b = l_s * w_s + l_b * w_b
                m_b = m_n
            if mode == "final":
                y_ref[rows, :] = (acc / l_b * jax.nn.silu(g_ref[rows, :])).astype(y_ref.dtype)
            else:
                st_state(m_ref, rows, m_b)
                st_state(l_ref, rows, l_b)
                st_state(acc_ref, rows, acc)

    GROUP = 2

    def spec16(r):
        rows = pl.ds(r, ATTN_BLOCK, stride=16)
        return (2, first_tile, rows, ld(kp_ref, rows), ld(kc_ref, rows),
                ld(vp_ref, rows), ld(vc_ref, rows))

    def body16(i, carry):
        run_blocks("init", [spec16(GROUP * i + j) for j in range(GROUP)])
        return carry
    lax.fori_loop(0, 16 // GROUP, body16, 0)

    span = 4 * ATTN_BLOCK

    def spec4(nb, r):
        rows = pl.ds(nb * span + r, ATTN_BLOCK, stride=4)
        if nb == 0:
            prev = pl.ds(ATTN_TILE - span + r, ATTN_BLOCK, stride=4)
            return (1, first_tile, rows, ld(kp_ref, prev), ld(kc_ref, rows),
                    ld(vp_ref, prev), ld(vc_ref, rows))
        prev = pl.ds((nb - 1) * span + r, ATTN_BLOCK, stride=4)
        return (1, 0, rows, ld(kc_ref, prev), ld(kc_ref, rows),
                ld(vc_ref, prev), ld(vc_ref, rows))

    for nb in range(ATTN_TILE // span):
        def body4(i, carry, nb=nb):
            run_blocks("merge", [spec4(nb, GROUP * i + j) for j in range(GROUP)])
            return carry
        lax.fori_loop(0, 4 // GROUP, body4, 0)

    def spec1(nb):
        start = pl.multiple_of(nb * ATTN_BLOCK, ATTN_BLOCK)
        rows = pl.ds(start, ATTN_BLOCK)
        prev = pl.ds(start - ATTN_BLOCK, ATTN_BLOCK)
        return (0, 0, rows, ld(kc_ref, prev), ld(kc_ref, rows),
                ld(vc_ref, prev), ld(vc_ref, rows))

    rows0 = pl.ds(0, ATTN_BLOCK)
    prev0 = pl.ds(ATTN_TILE - ATTN_BLOCK, ATTN_BLOCK)
    run_blocks("final", [(0, first_tile, rows0, ld(kp_ref, prev0), ld(kc_ref, rows0),
                          ld(vp_ref, prev0), ld(vc_ref, rows0))]
               + [spec1(j) for j in range(1, GROUP)])

    def body1(i, carry):
        run_blocks("final", [spec1(GROUP * i + j) for j in range(GROUP)])
        return carry
    lax.fori_loop(1, ATTN_TILE // ATTN_BLOCK // GROUP, body1, 0)


def _attention(zq, batch, seq):
    tiles = seq // ATTN_TILE
    halves = GROUP_W // LANES

    def cur(c):
        return [pl.BlockSpec((ATTN_TILE, LANES), lambda b, n, c=c, h=h: (b * tiles + n, c * halves + h))
                for h in range(halves)]

    def prev(c):
        return [pl.BlockSpec((ATTN_TILE, LANES),
                             lambda b, n, c=c, h=h: (b * tiles + jnp.maximum(n - 1, 0), c * halves + h))
                for h in range(halves)]

    in_specs = cur(0) + prev(1) + cur(1) + prev(2) + cur(2) + [
        pl.BlockSpec((ATTN_TILE, GROUP_W), lambda b, n: (b * tiles + n, 3))]
    state = pltpu.VMEM((halves, ATTN_TILE, LANES), F32)
    return pl.pallas_call(
        _attn_kernel,
        grid=(batch, tiles),
        in_specs=in_specs,
        out_specs=pl.BlockSpec((ATTN_TILE, GROUP_W), lambda b, n: (b * tiles + n, 0)),
        out_shape=jax.ShapeDtypeStruct((batch * seq, GROUP_W), BF16),
        scratch_shapes=[
            pltpu.VMEM((len(ATTN_DILATIONS) * HEADS * 2, ATTN_BLOCK, 2 * ATTN_BLOCK), F32),
            state, state, state,
        ],
        compiler_params=pltpu.CompilerParams(
            dimension_semantics=("arbitrary", "arbitrary"), vmem_limit_bytes=VMEM_LIMIT),
        name="dilated_attention",
    )(*([zq] * len(in_specs)))


def _block_diag(w):
    h, d, _ = w.shape
    eye = jnp.eye(h, dtype=w.dtype)
    return jnp.einsum("hij,hg->higj", w, eye).reshape(h * d, h * d)


def kernel(x, norm_g, w_in, conv_a_w, conv_r_w, conv_r_b, lru_wa, lru_ba, lru_wx, lru_bx,
           lru_lambda, gmlp_norm_g, gmlp_ws, gmlp_bs, w_out, final_g):
    batch, seq, _ = x.shape
    depth = w_in.shape[0]
    x2 = x.reshape(batch * seq, D_MODEL)
    row = lambda v: v.reshape(1, -1)
    w_in16 = w_in.astype(BF16)
    w_out16 = w_out.astype(BF16)
    prev = None
    for l in range(depth):
        prm = [
            conv_a_w[l], conv_r_w[l], row(conv_r_b[l]),
            _block_diag(lru_wa[l]).astype(BF16), row(lru_ba[l]),
            _block_diag(lru_wx[l]).astype(BF16), row(lru_bx[l]),
            row(lru_lambda[l]), row(gmlp_norm_g[l]),
            jnp.transpose(gmlp_ws[l], (1, 0, 2)).reshape(GMLP_CHUNK, HEADS * GMLP_CHUNK),
            jnp.repeat(jnp.transpose(gmlp_bs[l]), HEAD_DIM, axis=1),
        ]
        outs = _layer(batch, seq, x2, prev, row(norm_g[l]), w_in16[l], prm)
        if prev is not None:
            x2 = outs[0]
        zq, y_abc = outs[-2:]
        y_d = _attention(zq, batch, seq)
        prev = (y_abc, y_d, w_out16[l, :D_MIX_ABC], w_out16[l, D_MIX_ABC:])
    out = _out_final(x2, *prev, row(final_g))
    return out.reshape(batch, seq, D_MODEL)
```

```python
import functools
import math

import jax
import jax.numpy as jnp
from jax import lax
from jax.experimental import pallas as pl
from jax.experimental.pallas import tpu as pltpu

F32 = jnp.float32
BF16 = jnp.bfloat16

D_MODEL = 1024
GROUP_W = 256
HEADS = 4
HEAD_DIM = 64
N_CHUNKS = 13
N_MIX_CHUNKS = 9
D_IN = N_CHUNKS * GROUP_W
D_MIX_ABC = 3 * GROUP_W
D_ATTN = (N_CHUNKS - N_MIX_CHUNKS) * GROUP_W
RG_C = 8.0
GMLP_CHUNK = 128
ATTN_BLOCK = 128
LANES = 128
ATTN_DILATIONS = (1, 4, 16)
ATTN_TILE = ATTN_BLOCK * 16
GROUP_STRIDE16 = 2
GROUP_CONTIGUOUS = 4
NORM_EPS = 1e-6
MASKED = -1e30
ATTN_SCALE = 1.0 / math.sqrt(HEAD_DIM)
assert math.frexp(ATTN_SCALE)[0] == 0.5, "softmax scale must be a power of two to fold into q"

LAYER_ROWS = 512
SUBLANES = 8
CONV_PAD = SUBLANES
VMEM_LIMIT = 56 * 1024 * 1024


def _rms(x, g):
    return x * lax.rsqrt(jnp.mean(x * x, axis=-1, keepdims=True) + NORM_EPS) * g


def _lower_head_mask(rows):
    return lax.broadcasted_iota(jnp.int32, (rows, LANES), 1) < HEAD_DIM


def _only_head(x, hh):
    lower_head = _lower_head_mask(x.shape[0])
    half, lower = hh // 2, hh % 2 == 0
    part = x[:, half * LANES:(half + 1) * LANES]
    part = jnp.where(lower_head, part, 0.0) if lower else jnp.where(lower_head, 0.0, part)
    zeros = jnp.zeros_like(part)
    return jnp.concatenate([part if i == half else zeros for i in range(HEADS // 2)], axis=1)


def _causal_conv(carry_ref, x, w_ref):
    rows = x.shape[0]
    taps = w_ref.shape[0]
    history = carry_ref[...]
    carry_ref[...] = x[rows - CONV_PAD:, :]
    row = lax.broadcasted_iota(jnp.int32, history.shape, 0)
    out = x * w_ref[taps - 1:taps, :]
    for k in range(taps - 1):
        shift = taps - 1 - k
        rolled = pltpu.roll(x, shift, axis=0)
        head = jnp.where(row < shift, pltpu.roll(history, shift, axis=0), rolled[0:CONV_PAD])
        shifted = jnp.concatenate([head, rolled[CONV_PAD:]], axis=0)
        out += shifted * w_ref[k:k + 1, :]
    return out


def _linear_scan(a, b, h0):
    rows, cols = a.shape
    groups = rows // SUBLANES
    a = a.reshape(groups, SUBLANES, cols)
    b = b.reshape(groups, SUBLANES, cols)
    row = lax.broadcasted_iota(jnp.int32, a.shape, 1)
    d = 1
    while d < SUBLANES:
        keep = row >= d
        a_prev = jnp.where(keep, pltpu.roll(a, d, axis=1), 1.0)
        b_prev = jnp.where(keep, pltpu.roll(b, d, axis=1), 0.0)
        b = a * b_prev + b
        a = a * a_prev
        d *= 2
    out = []
    for g in range(groups):
        hg = b[g] + a[g] * h0
        out.append(hg)
        h0 = hg[SUBLANES - 1:SUBLANES, :]
    return jnp.concatenate(out, axis=0)


def _mixers_abc(z, emit_other_matmuls, prm, y_ref, carry_a, carry_r, carry_h):
    (conv_a_w_ref, conv_r_w_ref, conv_r_b_ref, wa_ref, ba_ref, wx_ref, bx_ref,
     lam_ref, gmlp_g_ref, ws_ref, bs_ref) = prm
    r_x, c_v = z(4), z(7)
    a_x, a_b, a_c, a_g = z(0), z(1), z(2), z(3)
    r_g, c_u, c_g = z(5), z(6), z(8)
    rows = a_x.shape[0]

    conv_a = _causal_conv(carry_a, a_c * a_x, conv_a_w_ref)
    y_ref[:, 0:GROUP_W] = (a_b * conv_a * jax.nn.silu(a_g)).astype(y_ref.dtype)

    xb = _causal_conv(carry_r, r_x, conv_r_w_ref) + conv_r_b_ref[...]
    xb16 = xb.astype(BF16)
    r = jax.nn.sigmoid(jnp.dot(xb16, wa_ref[...], preferred_element_type=F32) + ba_ref[...])
    i = jax.nn.sigmoid(jnp.dot(xb16, wx_ref[...], preferred_element_type=F32) + bx_ref[...])
    log_a = (-RG_C * r) * jax.nn.softplus(-lam_ref[...])
    a = jnp.exp(log_a)
    th = jnp.tanh(log_a)
    mult = jnp.sqrt(-2.0 * th / (1.0 - th))
    h = _linear_scan(a, mult * (i * xb), carry_h[0:1, :])
    carry_h[...] = jnp.broadcast_to(h[rows - 1:rows, :], carry_h.shape)
    y_ref[:, GROUP_W:2 * GROUP_W] = (h * jax.nn.silu(r_g)).astype(y_ref.dtype)

    vv = _rms(jax.nn.gelu(c_v), gmlp_g_ref[...])
    t_idx = lax.broadcasted_iota(jnp.int32, ws_ref.shape, 0)
    s_idx = lax.broadcasted_iota(jnp.int32, ws_ref.shape, 1) % GMLP_CHUNK
    ws = jnp.where(s_idx <= t_idx, ws_ref[...], 0.0).astype(BF16)
    emit_other_matmuls()
    for j in range(rows // GMLP_CHUNK):
        sl = slice(j * GMLP_CHUNK, (j + 1) * GMLP_CHUNK)
        vj = vv[sl, :]
        v_heads = jnp.concatenate([_only_head(vj, hh) for hh in range(HEADS)], axis=0).astype(BF16)
        sp = jnp.dot(ws, v_heads, preferred_element_type=F32) + bs_ref[...]
        yc = jax.nn.gelu(c_u[sl, :]) * sp * jax.nn.silu(c_g[sl, :])
        y_ref[sl, 2 * GROUP_W:3 * GROUP_W] = yc.astype(y_ref.dtype)


N_MIX_PARAMS = 11


def _residual(x_ref, yabc_ref, yd_ref, w1_ref, w2_ref):
    acc = jnp.dot(yabc_ref[...], w1_ref[...], preferred_element_type=F32)
    acc += jnp.dot(yd_ref[...], w2_ref[...], preferred_element_type=F32)
    return x_ref[...] + acc


def _layer_kernel(has_residual, *refs):
    refs = list(refs)
    take = lambda n: [refs.pop(0) for _ in range(n)]
    if has_residual:
        x_ref, yabc_ref, yd_ref, w1_ref, w2_ref = take(5)
    else:
        (x_ref,) = take(1)
    g_ref, w_ref = take(2)
    prm = take(N_MIX_PARAMS)
    if has_residual:
        (xo_ref,) = take(1)
    zq_ref, y_ref, carry_a, carry_r, carry_h = refs

    @pl.when(pl.program_id(1) == 0)
    def _():
        carry_a[...] = jnp.zeros_like(carry_a)
        carry_r[...] = jnp.zeros_like(carry_r)
        carry_h[...] = jnp.zeros_like(carry_h)

    if has_residual:
        x = _residual(x_ref, yabc_ref, yd_ref, w1_ref, w2_ref)
        xo_ref[...] = x
    else:
        x = x_ref[...]
    h = _rms(x, g_ref[...]).astype(BF16)

    def z(c):
        cols = slice(c * GROUP_W, (c + 1) * GROUP_W)
        return jnp.dot(h, w_ref[:, cols], preferred_element_type=F32)

    def attention_chunks():
        zq_ref[:, 0:GROUP_W] = z(N_MIX_CHUNKS) * ATTN_SCALE
        for c in range(N_MIX_CHUNKS + 1, N_CHUNKS):
            cols = slice((c - N_MIX_CHUNKS) * GROUP_W, (c - N_MIX_CHUNKS + 1) * GROUP_W)
            zq_ref[:, cols] = z(c)

    _mixers_abc(z, attention_chunks, prm, y_ref, carry_a, carry_r, carry_h)


def _layer(batch, seq, x2, prev, g, w_in16, prm):
    tiles = seq // LAYER_ROWS
    n = batch * seq
    has_residual = prev is not None

    def rows(width):
        return pl.BlockSpec((LAYER_ROWS, width), lambda b, t: (b * tiles + t, 0))

    def full(a):
        return pl.BlockSpec(a.shape, lambda b, t: (0,) * a.ndim)

    args = [x2] + (list(prev) if has_residual else []) + [g, w_in16] + list(prm)
    in_specs = [rows(D_MODEL)]
    if has_residual:
        in_specs += [rows(D_MIX_ABC), rows(GROUP_W), full(prev[2]), full(prev[3])]
    in_specs += [full(g), full(w_in16)] + [full(a) for a in prm]
    out_specs = [rows(D_ATTN), rows(D_MIX_ABC)]
    out_shape = [jax.ShapeDtypeStruct((n, D_ATTN), F32),
                 jax.ShapeDtypeStruct((n, D_MIX_ABC), BF16)]
    if has_residual:
        out_specs = [rows(D_MODEL)] + out_specs
        out_shape = [jax.ShapeDtypeStruct((n, D_MODEL), F32)] + out_shape
    return pl.pallas_call(
        functools.partial(_layer_kernel, has_residual),
        grid=(batch, tiles),
        in_specs=in_specs,
        out_specs=out_specs,
        out_shape=out_shape,
        scratch_shapes=[
            pltpu.VMEM((CONV_PAD, GROUP_W), F32),
            pltpu.VMEM((CONV_PAD, GROUP_W), F32),
            pltpu.VMEM((SUBLANES, GROUP_W), F32),
        ],
        compiler_params=pltpu.CompilerParams(
            dimension_semantics=("arbitrary", "arbitrary"), vmem_limit_bytes=VMEM_LIMIT),
        name="layer_res" if has_residual else "layer_first",
    )(*args)


def _out_final_kernel(x_ref, yabc_ref, yd_ref, w1_ref, w2_ref, g_ref, o_ref):
    o_ref[...] = _rms(_residual(x_ref, yabc_ref, yd_ref, w1_ref, w2_ref), g_ref[...])


def _out_final(x2, y_abc, y_d, w1, w2, g):
    n = x2.shape[0]
    rows = lambda width: pl.BlockSpec((LAYER_ROWS, width), lambda i: (i, 0))
    full = lambda a: pl.BlockSpec(a.shape, lambda i: (0,) * a.ndim)
    return pl.pallas_call(
        _out_final_kernel,
        grid=(n // LAYER_ROWS,),
        in_specs=[rows(D_MODEL), rows(D_MIX_ABC), rows(GROUP_W), full(w1), full(w2), full(g)],
        out_specs=rows(D_MODEL),
        out_shape=jax.ShapeDtypeStruct((n, D_MODEL), F32),
        compiler_params=pltpu.CompilerParams(
            dimension_semantics=("arbitrary",), vmem_limit_bytes=VMEM_LIMIT),
        name="out_final",
    )(x2, y_abc, y_d, w1, w2, g)


def _attn_kernel(q0, q1, kp0, kp1, kc0, kc1, vp0, vp1, vc0, vc1, g_ref, y_ref,
                 bias_ref, acc_ref, m_ref, l_ref):
    q_ref, kp_ref, kc_ref, vp_ref, vc_ref = (q0, q1), (kp0, kp1), (kc0, kc1), (vp0, vp1), (vc0, vc1)
    n = pl.program_id(1)
    first_tile = (n == 0).astype(jnp.int32)

    def ld(halves, rows):
        return jnp.concatenate([h[rows, :] for h in halves], axis=1)

    def ld_state(ref, rows):
        return jnp.concatenate([ref[0, rows, :], ref[1, rows, :]], axis=1)

    def st_state(ref, rows, val):
        ref[0, rows, :] = val[:, :LANES]
        ref[1, rows, :] = val[:, LANES:]

    qi = lax.broadcasted_iota(jnp.int32, (ATTN_BLOCK, 2 * ATTN_BLOCK), 0)
    kj = lax.broadcasted_iota(jnp.int32, (ATTN_BLOCK, 2 * ATTN_BLOCK), 1)
    delta = qi + ATTN_BLOCK - kj
    band = (delta >= 0) & (delta <= ATTN_BLOCK)
    for p, dil in enumerate(ATTN_DILATIONS):
        for hh in range(HEADS):
            slope = 2.0 ** (-8.0 * (hh + 1) / HEADS)
            bias = -slope * (delta * dil).astype(F32)
            bias_ref[(p * HEADS + hh) * 2] = jnp.where(band, bias, MASKED)
            bias_ref[(p * HEADS + hh) * 2 + 1] = jnp.where(
                band & (kj >= ATTN_BLOCK), bias, MASKED)

    lower_head = _lower_head_mask(ATTN_BLOCK)

    def expand(cols):
        return jnp.concatenate(
            [jnp.where(lower_head, cols[2 * i], cols[2 * i + 1]) for i in range(HEADS // 2)], axis=1)

    def pick_heads(per_head):
        return jnp.concatenate(
            [jnp.where(lower_head, per_head[2 * i][:, i * LANES:(i + 1) * LANES],
                       per_head[2 * i + 1][:, i * LANES:(i + 1) * LANES])
             for i in range(HEADS // 2)], axis=1)

    def run_blocks(mode, specs):
        scores = []
        for p, first, rows, k_prev, k_cur, v_prev, v_cur in specs:
            q = ld(q_ref, rows)
            kk = jnp.concatenate([k_prev, k_cur], axis=0).astype(BF16)
            q_heads = jnp.concatenate([_only_head(q, hh) for hh in range(HEADS)],
                                      axis=0).astype(BF16)
            scores.append(lax.dot_general(q_heads, kk, (((1,), (1,)), ((), ())),
                                          preferred_element_type=F32))
        soft = []
        for (p, first, *_), s_all in zip(specs, scores):
            probs, m_cols, l_cols = [], [], []
            for hh in range(HEADS):
                s = s_all[hh * ATTN_BLOCK:(hh + 1) * ATTN_BLOCK, :]
                s = s + bias_ref[(p * HEADS + hh) * 2 + first]
                m_h = jnp.max(s, axis=-1, keepdims=True)
                e = jnp.exp(s - m_h)
                probs.append(e.astype(BF16))
                m_cols.append(m_h)
                l_cols.append(jnp.sum(e, axis=-1, keepdims=True))
            soft.append((jnp.concatenate(probs, axis=0), m_cols, l_cols))
        for (p, first, rows, k_prev, k_cur, v_prev, v_cur), (p_all, m_cols, l_cols) in zip(specs, soft):
            vv = jnp.concatenate([v_prev, v_cur], axis=0).astype(BF16)
            o_all = jnp.dot(p_all, vv, preferred_element_type=F32)
            acc = pick_heads([o_all[hh * ATTN_BLOCK:(hh + 1) * ATTN_BLOCK, :] for hh in range(HEADS)])
            m_b = expand(m_cols)
            l_b = expand(l_cols)
            if mode != "init":
                m_s, l_s, acc_s = ld_state(m_ref, rows), ld_state(l_ref, rows), ld_state(acc_ref, rows)
                m_n = jnp.maximum(m_s, m_b)
                w_s = jnp.exp(m_s - m_n)
                w_b = jnp.exp(m_b - m_n)
                acc = acc_s * w_s + acc * w_b
                l_b = l_s * w_s + l_b * w_b
                m_b = m_n
            if mode == "final":
                y_ref[rows, :] = (acc / l_b * jax.nn.silu(g_ref[rows, :])).astype(y_ref.dtype)
            else:
                st_state(m_ref, rows, m_b)
                st_state(l_ref, rows, l_b)
                st_state(acc_ref, rows, acc)


    def spec16(r):
        rows = pl.ds(r, ATTN_BLOCK, stride=16)
        return (2, first_tile, rows, ld(kp_ref, rows), ld(kc_ref, rows),
                ld(vp_ref, rows), ld(vc_ref, rows))

    def body16(i, carry):
        run_blocks("init", [spec16(GROUP_STRIDE16 * i + j) for j in range(GROUP_STRIDE16)])
        return carry
    lax.fori_loop(0, 16 // GROUP_STRIDE16, body16, 0)

    span = 4 * ATTN_BLOCK

    def spec4(nb, r):
        rows = pl.ds(nb * span + r, ATTN_BLOCK, stride=4)
        if nb == 0:
            prev = pl.ds(ATTN_TILE - span + r, ATTN_BLOCK, stride=4)
            return (1, first_tile, rows, ld(kp_ref, prev), ld(kc_ref, rows),
                    ld(vp_ref, prev), ld(vc_ref, rows))
        prev = pl.ds((nb - 1) * span + r, ATTN_BLOCK, stride=4)
        return (1, 0, rows, ld(kc_ref, prev), ld(kc_ref, rows),
                ld(vc_ref, prev), ld(vc_ref, rows))

    for nb in range(ATTN_TILE // span):
        run_blocks("merge", [spec4(nb, r) for r in range(4)])

    def spec1(nb):
        start = pl.multiple_of(nb * ATTN_BLOCK, ATTN_BLOCK)
        rows = pl.ds(start, ATTN_BLOCK)
        prev = pl.ds(start - ATTN_BLOCK, ATTN_BLOCK)
        return (0, 0, rows, ld(kc_ref, prev), ld(kc_ref, rows),
                ld(vc_ref, prev), ld(vc_ref, rows))

    rows0 = pl.ds(0, ATTN_BLOCK)
    prev0 = pl.ds(ATTN_TILE - ATTN_BLOCK, ATTN_BLOCK)
    run_blocks("final", [(0, first_tile, rows0, ld(kp_ref, prev0), ld(kc_ref, rows0),
                          ld(vp_ref, prev0), ld(vc_ref, rows0))]
               + [spec1(j) for j in range(1, GROUP_CONTIGUOUS)])

    def body1(i, carry):
        run_blocks("final", [spec1(GROUP_CONTIGUOUS * i + j) for j in range(GROUP_CONTIGUOUS)])
        return carry
    lax.fori_loop(1, ATTN_TILE // ATTN_BLOCK // GROUP_CONTIGUOUS, body1, 0)


def _attention(zq, batch, seq):
    tiles = seq // ATTN_TILE
    halves = GROUP_W // LANES

    def cur(c):
        return [pl.BlockSpec((ATTN_TILE, LANES), lambda b, n, c=c, h=h: (b * tiles + n, c * halves + h))
                for h in range(halves)]

    def prev(c):
        return [pl.BlockSpec((ATTN_TILE, LANES),
                             lambda b, n, c=c, h=h: (b * tiles + jnp.maximum(n - 1, 0), c * halves + h))
                for h in range(halves)]

    in_specs = cur(0) + prev(1) + cur(1) + prev(2) + cur(2) + [
        pl.BlockSpec((ATTN_TILE, GROUP_W), lambda b, n: (b * tiles + n, 3))]
    state = pltpu.VMEM((halves, ATTN_TILE, LANES), F32)
    return pl.pallas_call(
        _attn_kernel,
        grid=(batch, tiles),
        in_specs=in_specs,
        out_specs=pl.BlockSpec((ATTN_TILE, GROUP_W), lambda b, n: (b * tiles + n, 0)),
        out_shape=jax.ShapeDtypeStruct((batch * seq, GROUP_W), BF16),
        scratch_shapes=[
            pltpu.VMEM((len(ATTN_DILATIONS) * HEADS * 2, ATTN_BLOCK, 2 * ATTN_BLOCK), F32),
            state, state, state,
        ],
        compiler_params=pltpu.CompilerParams(
            dimension_semantics=("arbitrary", "arbitrary"), vmem_limit_bytes=VMEM_LIMIT),
        name="dilated_attention",
    )(*([zq] * len(in_specs)))


def _block_diag(w):
    h, d, _ = w.shape
    eye = jnp.eye(h, dtype=w.dtype)
    return jnp.einsum("hij,hg->higj", w, eye).reshape(h * d, h * d)


def kernel(x, norm_g, w_in, conv_a_w, conv_r_w, conv_r_b, lru_wa, lru_ba, lru_wx, lru_bx,
           lru_lambda, gmlp_norm_g, gmlp_ws, gmlp_bs, w_out, final_g):
    batch, seq, _ = x.shape
    depth = w_in.shape[0]
    x2 = x.reshape(batch * seq, D_MODEL)
    row = lambda v: v.reshape(1, -1)
    w_in16 = w_in.astype(BF16)
    w_out16 = w_out.astype(BF16)
    prev = None
    for l in range(depth):
        prm = [
            conv_a_w[l], conv_r_w[l], row(conv_r_b[l]),
            _block_diag(lru_wa[l]).astype(BF16), row(lru_ba[l]),
            _block_diag(lru_wx[l]).astype(BF16), row(lru_bx[l]),
            row(lru_lambda[l]), row(gmlp_norm_g[l]),
            jnp.transpose(gmlp_ws[l], (1, 0, 2)).reshape(GMLP_CHUNK, HEADS * GMLP_CHUNK),
            jnp.repeat(jnp.transpose(gmlp_bs[l]), HEAD_DIM, axis=1),
        ]
        outs = _layer(batch, seq, x2, prev, row(norm_g[l]), w_in16[l], prm)
        if prev is not None:
            x2 = outs[0]
        zq, y_abc = outs[-2:]
        y_d = _attention(zq, batch, seq)
        prev = (y_abc, y_d, w_out16[l, :D_MIX_ABC], w_out16[l, D_MIX_ABC:])
    out = _out_final(x2, *prev, row(final_g))
    return out.reshape(batch, seq, D_MODEL)
```

```python
import functools
import math

import jax
import jax.numpy as jnp
from jax import lax
from jax.experimental import pallas as pl
from jax.experimental.pallas import tpu as pltpu

F32 = jnp.float32
BF16 = jnp.bfloat16

D_MODEL = 1024
GROUP_W = 256
HEADS = 4
HEAD_DIM = 64
N_CHUNKS = 13
N_MIX_CHUNKS = 9
D_IN = N_CHUNKS * GROUP_W
D_MIX_ABC = 3 * GROUP_W
D_ATTN = (N_CHUNKS - N_MIX_CHUNKS) * GROUP_W
RG_C = 8.0
GMLP_CHUNK = 128
ATTN_BLOCK = 128
LANES = 128
ATTN_DILATIONS = (1, 4, 16)
ATTN_TILE = ATTN_BLOCK * 16
GROUP_STRIDE16 = 2
GROUP_CONTIGUOUS = 4
NORM_EPS = 1e-6
MASKED = -1e30
ATTN_SCALE = 1.0 / math.sqrt(HEAD_DIM)
assert math.frexp(ATTN_SCALE)[0] == 0.5, "softmax scale must be a power of two to fold into q"

LAYER_ROWS = 512
RESIDUES = 4
SUBTILE = LAYER_ROWS
assert ATTN_TILE % SUBTILE == 0 and SUBTILE == 4 * ATTN_BLOCK
SUBLANES = 8
CONV_PAD = SUBLANES
VMEM_LIMIT = 56 * 1024 * 1024


def _rms(x, g):
    return x * lax.rsqrt(jnp.mean(x * x, axis=-1, keepdims=True) + NORM_EPS) * g


def _lower_head_mask(rows):
    return lax.broadcasted_iota(jnp.int32, (rows, LANES), 1) < HEAD_DIM


def _only_head(x, hh):
    lower_head = _lower_head_mask(x.shape[0])
    half, lower = hh // 2, hh % 2 == 0
    part = x[:, half * LANES:(half + 1) * LANES]
    part = jnp.where(lower_head, part, 0.0) if lower else jnp.where(lower_head, 0.0, part)
    zeros = jnp.zeros_like(part)
    return jnp.concatenate([part if i == half else zeros for i in range(HEADS // 2)], axis=1)


def _to_residue_major(buf, x):
    per = x.shape[0] // RESIDUES
    for half in range(GROUP_W // LANES):
        buf[half] = x[:, half * LANES:(half + 1) * LANES]
    return jnp.concatenate(
        [jnp.concatenate([buf[half, pl.ds(b, per, stride=RESIDUES), :]
                          for half in range(GROUP_W // LANES)], axis=1)
         for b in range(RESIDUES)], axis=0)


def _from_residue_major(buf, x):
    per = x.shape[0] // RESIDUES
    for b in range(RESIDUES):
        for half in range(GROUP_W // LANES):
            buf[half, pl.ds(b, per, stride=RESIDUES), :] = (
                x[b * per:(b + 1) * per, half * LANES:(half + 1) * LANES])
    return jnp.concatenate([buf[half] for half in range(GROUP_W // LANES)], axis=1)


def _causal_conv(carry_ref, x, w_ref):
    rows = x.shape[0]
    taps = w_ref.shape[0]
    history = carry_ref[...]
    carry_ref[...] = x[rows - CONV_PAD:, :]
    row = lax.broadcasted_iota(jnp.int32, history.shape, 0)
    out = x * w_ref[taps - 1:taps, :]
    for k in range(taps - 1):
        shift = taps - 1 - k
        rolled = pltpu.roll(x, shift, axis=0)
        head = jnp.where(row < shift, pltpu.roll(history, shift, axis=0), rolled[0:CONV_PAD])
        shifted = jnp.concatenate([head, rolled[CONV_PAD:]], axis=0)
        out += shifted * w_ref[k:k + 1, :]
    return out


def _linear_scan(a, b, h0):
    rows, cols = a.shape
    groups = rows // SUBLANES
    a = a.reshape(groups, SUBLANES, cols)
    b = b.reshape(groups, SUBLANES, cols)
    row = lax.broadcasted_iota(jnp.int32, a.shape, 1)
    d = 1
    while d < SUBLANES:
        keep = row >= d
        a_prev = jnp.where(keep, pltpu.roll(a, d, axis=1), 1.0)
        b_prev = jnp.where(keep, pltpu.roll(b, d, axis=1), 0.0)
        b = a * b_prev + b
        a = a * a_prev
        d *= 2
    out = []
    for g in range(groups):
        hg = b[g] + a[g] * h0
        out.append(hg)
        h0 = hg[SUBLANES - 1:SUBLANES, :]
    return jnp.concatenate(out, axis=0)


def _mixers_abc(z, emit_other_matmuls, prm, y_ref, carry_a, carry_r, carry_h):
    (conv_a_w_ref, conv_r_w_ref, conv_r_b_ref, wa_ref, ba_ref, wx_ref, bx_ref,
     lam_ref, gmlp_g_ref, ws_ref, bs_ref) = prm
    r_x, c_v = z(4), z(7)
    a_x, a_b, a_c, a_g = z(0), z(1), z(2), z(3)
    r_g, c_u, c_g = z(5), z(6), z(8)
    rows = a_x.shape[0]

    conv_a = _causal_conv(carry_a, a_c * a_x, conv_a_w_ref)
    y_ref[:, 0:GROUP_W] = (a_b * conv_a * jax.nn.silu(a_g)).astype(y_ref.dtype)

    xb = _causal_conv(carry_r, r_x, conv_r_w_ref) + conv_r_b_ref[...]
    xb16 = xb.astype(BF16)
    r = jax.nn.sigmoid(jnp.dot(xb16, wa_ref[...], preferred_element_type=F32) + ba_ref[...])
    i = jax.nn.sigmoid(jnp.dot(xb16, wx_ref[...], preferred_element_type=F32) + bx_ref[...])
    log_a = (-RG_C * r) * jax.nn.softplus(-lam_ref[...])
    a = jnp.exp(log_a)
    th = jnp.tanh(log_a)
    mult = jnp.sqrt(-2.0 * th / (1.0 - th))
    h = _linear_scan(a, mult * (i * xb), carry_h[0:1, :])
    carry_h[...] = jnp.broadcast_to(h[rows - 1:rows, :], carry_h.shape)
    y_ref[:, GROUP_W:2 * GROUP_W] = (h * jax.nn.silu(r_g)).astype(y_ref.dtype)

    vv = _rms(jax.nn.gelu(c_v), gmlp_g_ref[...])
    t_idx = lax.broadcasted_iota(jnp.int32, ws_ref.shape, 0)
    s_idx = lax.broadcasted_iota(jnp.int32, ws_ref.shape, 1) % GMLP_CHUNK
    ws = jnp.where(s_idx <= t_idx, ws_ref[...], 0.0).astype(BF16)
    emit_other_matmuls()
    for j in range(rows // GMLP_CHUNK):
        sl = slice(j * GMLP_CHUNK, (j + 1) * GMLP_CHUNK)
        vj = vv[sl, :]
        v_heads = jnp.concatenate([_only_head(vj, hh) for hh in range(HEADS)], axis=0).astype(BF16)
        sp = jnp.dot(ws, v_heads, preferred_element_type=F32) + bs_ref[...]
        yc = jax.nn.gelu(c_u[sl, :]) * sp * jax.nn.silu(c_g[sl, :])
        y_ref[sl, 2 * GROUP_W:3 * GROUP_W] = yc.astype(y_ref.dtype)


N_MIX_PARAMS = 11


def _residual(x_ref, yabc_ref, yd_ref, w1_ref, w2_ref, perm_buf):
    y_d = _from_residue_major(perm_buf, yd_ref[...].astype(F32)).astype(BF16)
    acc = jnp.dot(yabc_ref[...], w1_ref[...], preferred_element_type=F32)
    acc += jnp.dot(y_d, w2_ref[...], preferred_element_type=F32)
    return x_ref[...] + acc


def _layer_kernel(has_residual, *refs):
    refs = list(refs)
    take = lambda n: [refs.pop(0) for _ in range(n)]
    if has_residual:
        x_ref, yabc_ref, yd_ref, w1_ref, w2_ref = take(5)
    else:
        (x_ref,) = take(1)
    g_ref, w_ref = take(2)
    prm = take(N_MIX_PARAMS)
    if has_residual:
        (xo_ref,) = take(1)
    zq_ref, y_ref, carry_a, carry_r, carry_h, perm_buf = refs

    @pl.when(pl.program_id(1) == 0)
    def _():
        carry_a[...] = jnp.zeros_like(carry_a)
        carry_r[...] = jnp.zeros_like(carry_r)
        carry_h[...] = jnp.zeros_like(carry_h)

    if has_residual:
        x = _residual(x_ref, yabc_ref, yd_ref, w1_ref, w2_ref, perm_buf.at[N_CHUNKS - N_MIX_CHUNKS])
        xo_ref[...] = x
    else:
        x = x_ref[...]
    h = _rms(x, g_ref[...]).astype(BF16)

    def z(c):
        cols = slice(c * GROUP_W, (c + 1) * GROUP_W)
        return jnp.dot(h, w_ref[:, cols], preferred_element_type=F32)

    def attention_chunks():
        for k, c in enumerate(range(N_MIX_CHUNKS, N_CHUNKS)):
            zc = z(c) * ATTN_SCALE if k == 0 else z(c)
            zq_ref[:, k * GROUP_W:(k + 1) * GROUP_W] = _to_residue_major(perm_buf.at[k], zc)

    _mixers_abc(z, attention_chunks, prm, y_ref, carry_a, carry_r, carry_h)


def _layer(batch, seq, x2, prev, g, w_in16, prm):
    tiles = seq // LAYER_ROWS
    n = batch * seq
    has_residual = prev is not None

    def rows(width):
        return pl.BlockSpec((LAYER_ROWS, width), lambda b, t: (b * tiles + t, 0))

    def full(a):
        return pl.BlockSpec(a.shape, lambda b, t: (0,) * a.ndim)

    args = [x2] + (list(prev) if has_residual else []) + [g, w_in16] + list(prm)
    in_specs = [rows(D_MODEL)]
    if has_residual:
        in_specs += [rows(D_MIX_ABC), rows(GROUP_W), full(prev[2]), full(prev[3])]
    in_specs += [full(g), full(w_in16)] + [full(a) for a in prm]
    out_specs = [rows(D_ATTN), rows(D_MIX_ABC)]
    out_shape = [jax.ShapeDtypeStruct((n, D_ATTN), F32),
                 jax.ShapeDtypeStruct((n, D_MIX_ABC), BF16)]
    if has_residual:
        out_specs = [rows(D_MODEL)] + out_specs
        out_shape = [jax.ShapeDtypeStruct((n, D_MODEL), F32)] + out_shape
    return pl.pallas_call(
        functools.partial(_layer_kernel, has_residual),
        grid=(batch, tiles),
        in_specs=in_specs,
        out_specs=out_specs,
        out_shape=out_shape,
        scratch_shapes=[
            pltpu.VMEM((CONV_PAD, GROUP_W), F32),
            pltpu.VMEM((CONV_PAD, GROUP_W), F32),
            pltpu.VMEM((SUBLANES, GROUP_W), F32),
            pltpu.VMEM((N_CHUNKS - N_MIX_CHUNKS + 1, GROUP_W // LANES, LAYER_ROWS, LANES), F32),
        ],
        compiler_params=pltpu.CompilerParams(
            dimension_semantics=("arbitrary", "arbitrary"), vmem_limit_bytes=VMEM_LIMIT),
        name="layer_res" if has_residual else "layer_first",
    )(*args)


def _out_final_kernel(x_ref, yabc_ref, yd_ref, w1_ref, w2_ref, g_ref, o_ref, perm_buf):
    o_ref[...] = _rms(_residual(x_ref, yabc_ref, yd_ref, w1_ref, w2_ref, perm_buf), g_ref[...])


def _out_final(x2, y_abc, y_d, w1, w2, g):
    n = x2.shape[0]
    rows = lambda width: pl.BlockSpec((LAYER_ROWS, width), lambda i: (i, 0))
    full = lambda a: pl.BlockSpec(a.shape, lambda i: (0,) * a.ndim)
    return pl.pallas_call(
        _out_final_kernel,
        grid=(n // LAYER_ROWS,),
        in_specs=[rows(D_MODEL), rows(D_MIX_ABC), rows(GROUP_W), full(w1), full(w2), full(g)],
        out_specs=rows(D_MODEL),
        out_shape=jax.ShapeDtypeStruct((n, D_MODEL), F32),
        scratch_shapes=[pltpu.VMEM((GROUP_W // LANES, LAYER_ROWS, LANES), F32)],
        compiler_params=pltpu.CompilerParams(
            dimension_semantics=("arbitrary",), vmem_limit_bytes=VMEM_LIMIT),
        name="out_final",
    )(x2, y_abc, y_d, w1, w2, g)


def _attn_kernel(q0, q1, kp0, kp1, kc0, kc1, vp0, vp1, vc0, vc1, g_ref, y_ref,
                 bias_ref, acc_ref, m_ref, l_ref):
    q_ref, kp_ref, kc_ref, vp_ref, vc_ref = (q0, q1), (kp0, kp1), (kc0, kc1), (vp0, vp1), (vc0, vc1)
    first_tile = (pl.program_id(1) == 0).astype(jnp.int32)
    per = SUBTILE // RESIDUES

    def ld(halves, chunks):
        return jnp.concatenate(
            [jnp.concatenate([h[c, :] for h in halves], axis=1) for c, _ in chunks], axis=0)

    def ld_state(ref, chunks):
        return ld((ref.at[0], ref.at[1]), chunks)

    def st_rows(store, chunks, val):
        start = 0
        for c, size in chunks:
            store(c, val[start:start + size, :])
            start += size

    def st_state(ref, chunks, val):
        def store(c, v):
            ref[0, c, :] = v[:, :LANES]
            ref[1, c, :] = v[:, LANES:]
        st_rows(store, chunks, val)

    @pl.when((pl.program_id(0) == 0) & (pl.program_id(1) == 0))
    def _():
        qi = lax.broadcasted_iota(jnp.int32, (ATTN_BLOCK, 2 * ATTN_BLOCK), 0)
        kj = lax.broadcasted_iota(jnp.int32, (ATTN_BLOCK, 2 * ATTN_BLOCK), 1)
        q_major = RESIDUES * (qi % (per // RESIDUES)) + qi // (per // RESIDUES)
        k_major = (RESIDUES * (kj % (2 * per // RESIDUES) - per // RESIDUES)
                   + kj // (2 * per // RESIDUES))
        for p, dil in enumerate(ATTN_DILATIONS):
            steps = q_major - k_major if dil == 1 else qi + ATTN_BLOCK - kj
            in_block = k_major >= 0 if dil == 1 else kj >= ATTN_BLOCK
            band = (steps >= 0) & (steps <= ATTN_BLOCK)
            for hh in range(HEADS):
                slope = 2.0 ** (-8.0 * (hh + 1) / HEADS)
                bias = -slope * (steps * dil).astype(F32)
                bias_ref[(p * HEADS + hh) * 2] = jnp.where(band, bias, MASKED)
                bias_ref[(p * HEADS + hh) * 2 + 1] = jnp.where(band & in_block, bias, MASKED)

    lower_head = _lower_head_mask(ATTN_BLOCK)

    def expand(cols):
        return jnp.concatenate(
            [jnp.where(lower_head, cols[2 * i], cols[2 * i + 1]) for i in range(HEADS // 2)], axis=1)

    def pick_heads(per_head):
        return jnp.concatenate(
            [jnp.where(lower_head, per_head[2 * i][:, i * LANES:(i + 1) * LANES],
                       per_head[2 * i + 1][:, i * LANES:(i + 1) * LANES])
             for i in range(HEADS // 2)], axis=1)

    def run_blocks(mode, specs):
        scores = []
        for p, first, rows, kk, vv in specs:
            q = ld(q_ref, rows)
            q_heads = jnp.concatenate([_only_head(q, hh) for hh in range(HEADS)],
                                      axis=0).astype(BF16)
            scores.append(lax.dot_general(q_heads, kk.astype(BF16), (((1,), (1,)), ((), ())),
                                          preferred_element_type=F32))
        soft = []
        for (p, first, *_), s_all in zip(specs, scores):
            probs, m_cols, l_cols = [], [], []
            for hh in range(HEADS):
                s = s_all[hh * ATTN_BLOCK:(hh + 1) * ATTN_BLOCK, :]
                s = s + bias_ref[(p * HEADS + hh) * 2 + first]
                m_h = jnp.max(s, axis=-1, keepdims=True)
                e = jnp.exp(s - m_h)
                probs.append(e.astype(BF16))
                m_cols.append(m_h)
                l_cols.append(jnp.sum(e, axis=-1, keepdims=True))
            soft.append((jnp.concatenate(probs, axis=0), m_cols, l_cols))
        for (p, first, rows, kk, vv), (p_all, m_cols, l_cols) in zip(specs, soft):
            o_all = jnp.dot(p_all, vv.astype(BF16), preferred_element_type=F32)
            acc = pick_heads([o_all[hh * ATTN_BLOCK:(hh + 1) * ATTN_BLOCK, :] for hh in range(HEADS)])
            m_b = expand(m_cols)
            l_b = expand(l_cols)
            if mode != "init":
                m_s, l_s, acc_s = ld_state(m_ref, rows), ld_state(l_ref, rows), ld_state(acc_ref, rows)
                m_n = jnp.maximum(m_s, m_b)
                w_s = jnp.exp(m_s - m_n)
                w_b = jnp.exp(m_b - m_n)
                acc = acc_s * w_s + acc * w_b
                l_b = l_s * w_s + l_b * w_b
                m_b = m_n
            if mode == "final":
                gate = jnp.concatenate([g_ref[c, :] for c, _ in rows], axis=0)
                out = (acc / l_b * jax.nn.silu(gate)).astype(y_ref.dtype)

                def store(c, v):
                    y_ref[c, :] = v
                st_rows(store, rows, out)
            else:
                st_state(m_ref, rows, m_b)
                st_state(l_ref, rows, l_b)
                st_state(acc_ref, rows, acc)

    def keys_values(prev_refs, prev_rows, rows):
        k_src, v_src = (kp_ref, vp_ref) if prev_refs else (kc_ref, vc_ref)
        kk = jnp.concatenate([ld(k_src, prev_rows), ld(kc_ref, rows)], axis=0)
        vv = jnp.concatenate([ld(v_src, prev_rows), ld(vc_ref, rows)], axis=0)
        return kk, vv


    def spec16(r):
        a, b = r // RESIDUES, r % RESIDUES
        rows = [(pl.ds(t * SUBTILE + b * per + a, per // RESIDUES, stride=RESIDUES), per // RESIDUES)
                for t in range(ATTN_TILE // SUBTILE)]
        return (2, first_tile, rows) + keys_values(1, rows, rows)

    def body16(i, carry):
        run_blocks("init", [spec16(GROUP_STRIDE16 * i + j) for j in range(GROUP_STRIDE16)])
        return carry
    lax.fori_loop(0, 16 // GROUP_STRIDE16, body16, 0)

    def spec4(t, b):
        rows = [(pl.ds(t * SUBTILE + b * per, per), per)]
        t_prev = (t - 1) % (ATTN_TILE // SUBTILE)
        prev_rows = [(pl.ds(t_prev * SUBTILE + b * per, per), per)]
        first = first_tile if t == 0 else 0
        return (1, first, rows) + keys_values(t == 0, prev_rows, rows)

    for t in range(ATTN_TILE // SUBTILE):
        run_blocks("merge", [spec4(t, b) for b in range(RESIDUES)])

    n_sub = SUBTILE // ATTN_BLOCK
    blk = per // n_sub

    def spec1(t, u, t_static):
        base = t * SUBTILE
        if not isinstance(base, int):
            base = pl.multiple_of(base, SUBTILE)
        rows = [(pl.ds(base + b * per + u * blk, blk), blk) for b in range(RESIDUES)]
        if u > 0:
            key_rows = [(pl.ds(base + b * per + (u - 1) * blk, 2 * blk), 2 * blk)
                        for b in range(RESIDUES)]
            return (0, 0, rows, ld(kc_ref, key_rows), ld(vc_ref, key_rows))
        first_of_tile = t_static == 0
        k_src, v_src = (kp_ref, vp_ref) if first_of_tile else (kc_ref, vc_ref)
        prev_base = (ATTN_TILE - SUBTILE) if first_of_tile else base - SUBTILE
        kk, vv = [], []
        for b in range(RESIDUES):
            before = [(pl.ds(prev_base + b * per + per - blk, blk), blk)]
            kk += [ld(k_src, before), ld(kc_ref, rows[b:b + 1])]
            vv += [ld(v_src, before), ld(vc_ref, rows[b:b + 1])]
        first = first_tile if first_of_tile else 0
        return (0, first, rows, jnp.concatenate(kk, axis=0), jnp.concatenate(vv, axis=0))

    assert GROUP_CONTIGUOUS == n_sub
    run_blocks("final", [spec1(0, u, 0) for u in range(n_sub)])

    def body1(t, carry):
        run_blocks("final", [spec1(t, u, None) for u in range(n_sub)])
        return carry
    lax.fori_loop(1, ATTN_TILE // SUBTILE, body1, 0)


def _attention(zq, batch, seq):
    tiles = seq // ATTN_TILE
    halves = GROUP_W // LANES

    def cur(c):
        return [pl.BlockSpec((ATTN_TILE, LANES), lambda b, n, c=c, h=h: (b * tiles + n, c * halves + h))
                for h in range(halves)]

    def prev(c):
        return [pl.BlockSpec((ATTN_TILE, LANES),
                             lambda b, n, c=c, h=h: (b * tiles + jnp.maximum(n - 1, 0), c * halves + h))
                for h in range(halves)]

    in_specs = cur(0) + prev(1) + cur(1) + prev(2) + cur(2) + [
        pl.BlockSpec((ATTN_TILE, GROUP_W), lambda b, n: (b * tiles + n, 3))]
    state = pltpu.VMEM((halves, ATTN_TILE, LANES), F32)
    return pl.pallas_call(
        _attn_kernel,
        grid=(batch, tiles),
        in_specs=in_specs,
        out_specs=pl.BlockSpec((ATTN_TILE, GROUP_W), lambda b, n: (b * tiles + n, 0)),
        out_shape=jax.ShapeDtypeStruct((batch * seq, GROUP_W), BF16),
        scratch_shapes=[
            pltpu.VMEM((len(ATTN_DILATIONS) * HEADS * 2, ATTN_BLOCK, 2 * ATTN_BLOCK), F32),
            state, state, state,
        ],
        compiler_params=pltpu.CompilerParams(
            dimension_semantics=("arbitrary", "arbitrary"), vmem_limit_bytes=VMEM_LIMIT),
        name="dilated_attention",
    )(*([zq] * len(in_specs)))


def _block_diag(w):
    h, d, _ = w.shape
    eye = jnp.eye(h, dtype=w.dtype)
    return jnp.einsum("hij,hg->higj", w, eye).reshape(h * d, h * d)


def kernel(x, norm_g, w_in, conv_a_w, conv_r_w, conv_r_b, lru_wa, lru_ba, lru_wx, lru_bx,
           lru_lambda, gmlp_norm_g, gmlp_ws, gmlp_bs, w_out, final_g):
    batch, seq, _ = x.shape
    depth = w_in.shape[0]
    x2 = x.reshape(batch * seq, D_MODEL)
    row = lambda v: v.reshape(1, -1)
    w_in16 = w_in.astype(BF16)
    w_out16 = w_out.astype(BF16)
    prev = None
    for l in range(depth):
        prm = [
            conv_a_w[l], conv_r_w[l], row(conv_r_b[l]),
            _block_diag(lru_wa[l]).astype(BF16), row(lru_ba[l]),
            _block_diag(lru_wx[l]).astype(BF16), row(lru_bx[l]),
            row(lru_lambda[l]), row(gmlp_norm_g[l]),
            jnp.transpose(gmlp_ws[l], (1, 0, 2)).reshape(GMLP_CHUNK, HEADS * GMLP_CHUNK),
            jnp.repeat(jnp.transpose(gmlp_bs[l]), HEAD_DIM, axis=1),
        ]
        outs = _layer(batch, seq, x2, prev, row(norm_g[l]), w_in16[l], prm)
        if prev is not None:
            x2 = outs[0]
        zq, y_abc = outs[-2:]
        y_d = _attention(zq, batch, seq)
        prev = (y_abc, y_d, w_out16[l, :D_MIX_ABC], w_out16[l, D_MIX_ABC:])
    out = _out_final(x2, *prev, row(final_g))
    return out.reshape(batch, seq, D_MODEL)
```

```python
import functools
import math

import jax
import jax.numpy as jnp
from jax import lax
from jax.experimental import pallas as pl
from jax.experimental.pallas import tpu as pltpu

F32 = jnp.float32
BF16 = jnp.bfloat16

D_MODEL = 1024
GROUP_W = 256
HEADS = 4
HEAD_DIM = 64
N_CHUNKS = 13
N_MIX_CHUNKS = 9
D_IN = N_CHUNKS * GROUP_W
D_MIX_ABC = 3 * GROUP_W
D_ATTN = (N_CHUNKS - N_MIX_CHUNKS) * GROUP_W
RG_C = 8.0
GMLP_CHUNK = 128
ATTN_BLOCK = 128
LANES = 128
ATTN_DILATIONS = (1, 4, 16)
ATTN_TILE = ATTN_BLOCK * 16
GROUP_STRIDE16 = 4
GROUP_CONTIGUOUS = 4
NORM_EPS = 1e-6
MASKED = -1e30
ATTN_SCALE = 1.0 / math.sqrt(HEAD_DIM)
assert math.frexp(ATTN_SCALE)[0] == 0.5, "softmax scale must be a power of two to fold into q"

LAYER_ROWS = 512
RESIDUES = 4
SUBTILE = LAYER_ROWS
assert ATTN_TILE % SUBTILE == 0 and SUBTILE == 4 * ATTN_BLOCK
FINAL_ROWS = 2 * SUBTILE
SUBLANES = 8
CONV_PAD = SUBLANES
VMEM_LIMIT = 56 * 1024 * 1024


def _rms(x, g):
    return x * lax.rsqrt(jnp.mean(x * x, axis=-1, keepdims=True) + NORM_EPS) * g


def _lower_head_mask(rows):
    return lax.broadcasted_iota(jnp.int32, (rows, LANES), 1) < HEAD_DIM


def _only_head(x, hh):
    lower_head = _lower_head_mask(x.shape[0])
    half, lower = hh // 2, hh % 2 == 0
    part = x[:, half * LANES:(half + 1) * LANES]
    part = jnp.where(lower_head, part, 0.0) if lower else jnp.where(lower_head, 0.0, part)
    zeros = jnp.zeros_like(part)
    return jnp.concatenate([part if i == half else zeros for i in range(HEADS // 2)], axis=1)


def _to_residue_major(buf, x):
    per = x.shape[0] // RESIDUES
    for half in range(GROUP_W // LANES):
        buf[half] = x[:, half * LANES:(half + 1) * LANES]
    return jnp.concatenate(
        [jnp.concatenate([buf[half, pl.ds(b, per, stride=RESIDUES), :]
                          for half in range(GROUP_W // LANES)], axis=1)
         for b in range(RESIDUES)], axis=0)


def _from_residue_major(buf, x):
    per = SUBTILE // RESIDUES
    for start in range(0, x.shape[0], SUBTILE):
        for b in range(RESIDUES):
            for half in range(GROUP_W // LANES):
                buf[half, pl.ds(start + b, per, stride=RESIDUES), :] = (
                    x[start + b * per:start + (b + 1) * per, half * LANES:(half + 1) * LANES])
    return jnp.concatenate([buf[half] for half in range(GROUP_W // LANES)], axis=1)


def _causal_conv(carry_ref, x, w_ref):
    rows = x.shape[0]
    taps = w_ref.shape[0]
    history = carry_ref[...]
    carry_ref[...] = x[rows - CONV_PAD:, :]
    row = lax.broadcasted_iota(jnp.int32, history.shape, 0)
    out = x * w_ref[taps - 1:taps, :]
    for k in range(taps - 1):
        shift = taps - 1 - k
        rolled = pltpu.roll(x, shift, axis=0)
        head = jnp.where(row < shift, pltpu.roll(history, shift, axis=0), rolled[0:CONV_PAD])
        shifted = jnp.concatenate([head, rolled[CONV_PAD:]], axis=0)
        out += shifted * w_ref[k:k + 1, :]
    return out


def _linear_scan(a, b, h0):
    rows, cols = a.shape
    groups = rows // SUBLANES
    a = a.reshape(groups, SUBLANES, cols)
    b = b.reshape(groups, SUBLANES, cols)
    row = lax.broadcasted_iota(jnp.int32, a.shape, 1)
    d = 1
    while d < SUBLANES:
        keep = row >= d
        a_prev = jnp.where(keep, pltpu.roll(a, d, axis=1), 1.0)
        b_prev = jnp.where(keep, pltpu.roll(b, d, axis=1), 0.0)
        b = a * b_prev + b
        a = a * a_prev
        d *= 2
    out = []
    for g in range(groups):
        hg = b[g] + a[g] * h0
        out.append(hg)
        h0 = hg[SUBLANES - 1:SUBLANES, :]
    return jnp.concatenate(out, axis=0)


def _mixers_abc(z, emit_other_matmuls, prm, y_ref, carry_a, carry_r, carry_h):
    (conv_a_w_ref, conv_r_w_ref, conv_r_b_ref, wa_ref, ba_ref, wx_ref, bx_ref,
     lam_ref, gmlp_g_ref, ws_ref, bs_ref) = prm
    r_x, c_v = z(4), z(7)
    a_x, a_b, a_c, a_g = z(0), z(1), z(2), z(3)
    r_g, c_u, c_g = z(5), z(6), z(8)
    rows = a_x.shape[0]

    conv_a = _causal_conv(carry_a, a_c * a_x, conv_a_w_ref)
    y_ref[:, 0:GROUP_W] = (a_b * conv_a * jax.nn.silu(a_g)).astype(y_ref.dtype)

    xb = _causal_conv(carry_r, r_x, conv_r_w_ref) + conv_r_b_ref[...]
    xb16 = xb.astype(BF16)
    r = jax.nn.sigmoid(jnp.dot(xb16, wa_ref[...], preferred_element_type=F32) + ba_ref[...])
    i = jax.nn.sigmoid(jnp.dot(xb16, wx_ref[...], preferred_element_type=F32) + bx_ref[...])
    log_a = (-RG_C * r) * jax.nn.softplus(-lam_ref[...])
    a = jnp.exp(log_a)
    th = jnp.tanh(log_a)
    mult = jnp.sqrt(-2.0 * th / (1.0 - th))
    h = _linear_scan(a, mult * (i * xb), carry_h[0:1, :])
    carry_h[...] = jnp.broadcast_to(h[rows - 1:rows, :], carry_h.shape)
    y_ref[:, GROUP_W:2 * GROUP_W] = (h * jax.nn.silu(r_g)).astype(y_ref.dtype)

    vv = _rms(jax.nn.gelu(c_v), gmlp_g_ref[...])
    t_idx = lax.broadcasted_iota(jnp.int32, ws_ref.shape, 0)
    s_idx = lax.broadcasted_iota(jnp.int32, ws_ref.shape, 1) % GMLP_CHUNK
    ws = jnp.where(s_idx <= t_idx, ws_ref[...], 0.0).astype(BF16)
    emit_other_matmuls()
    for j in range(rows // GMLP_CHUNK):
        sl = slice(j * GMLP_CHUNK, (j + 1) * GMLP_CHUNK)
        vj = vv[sl, :]
        v_heads = jnp.concatenate([_only_head(vj, hh) for hh in range(HEADS)], axis=0).astype(BF16)
        sp = jnp.dot(ws, v_heads, preferred_element_type=F32) + bs_ref[...]
        yc = jax.nn.gelu(c_u[sl, :]) * sp * jax.nn.silu(c_g[sl, :])
        y_ref[sl, 2 * GROUP_W:3 * GROUP_W] = yc.astype(y_ref.dtype)


N_MIX_PARAMS = 11


def _residual(x_ref, yabc_ref, yd_ref, w1_ref, w2_ref, perm_buf):
    y_d = _from_residue_major(perm_buf, yd_ref[...].astype(F32)).astype(BF16)
    acc = jnp.dot(yabc_ref[...], w1_ref[...], preferred_element_type=F32)
    acc += jnp.dot(y_d, w2_ref[...], preferred_element_type=F32)
    return x_ref[...] + acc


def _layer_kernel(has_residual, *refs):
    refs = list(refs)
    take = lambda n: [refs.pop(0) for _ in range(n)]
    if has_residual:
        x_ref, yabc_ref, yd_ref, w1_f32, w2_f32 = take(5)
    else:
        (x_ref,) = take(1)
    g_ref, w_f32 = take(2)
    prm = take(N_MIX_PARAMS)
    if has_residual:
        (xo_ref,) = take(1)
    zq_ref, y_ref, carry_a, carry_r, carry_h, perm_buf, w_ref = refs[:7]
    if has_residual:
        w1_ref, w2_ref = refs[7:]

    @pl.when((pl.program_id(0) == 0) & (pl.program_id(1) == 0))
    def _():
        w_ref[...] = w_f32[...].astype(BF16)
        if has_residual:
            w1_ref[...] = w1_f32[...].astype(BF16)
            w2_ref[...] = w2_f32[...].astype(BF16)

    @pl.when(pl.program_id(1) == 0)
    def _():
        carry_a[...] = jnp.zeros_like(carry_a)
        carry_r[...] = jnp.zeros_like(carry_r)
        carry_h[...] = jnp.zeros_like(carry_h)

    if has_residual:
        x = _residual(x_ref, yabc_ref, yd_ref, w1_ref, w2_ref, perm_buf.at[N_CHUNKS - N_MIX_CHUNKS])
        xo_ref[...] = x
    else:
        x = x_ref[...]
    h = _rms(x, g_ref[...]).astype(BF16)

    def z(c):
        cols = slice(c * GROUP_W, (c + 1) * GROUP_W)
        return jnp.dot(h, w_ref[:, cols], preferred_element_type=F32)

    def attention_chunks():
        for k, c in enumerate(range(N_MIX_CHUNKS, N_CHUNKS)):
            zc = z(c) * ATTN_SCALE if k == 0 else z(c)
            zq_ref[:, k * GROUP_W:(k + 1) * GROUP_W] = _to_residue_major(perm_buf.at[k], zc)

    _mixers_abc(z, attention_chunks, prm, y_ref, carry_a, carry_r, carry_h)


def _resident(a, l):
    return pl.BlockSpec((None,) + a.shape[1:], lambda *_: (l,) + (0,) * (a.ndim - 1),
                        pipeline_mode=pl.Buffered(1))


def _w_out_specs(w_out, l):
    assert D_MIX_ABC % GROUP_W == 0
    return [pl.BlockSpec((None, D_MIX_ABC, D_MODEL), lambda *_: (l, 0, 0),
                         pipeline_mode=pl.Buffered(1)),
            pl.BlockSpec((None, GROUP_W, D_MODEL), lambda *_: (l, D_MIX_ABC // GROUP_W, 0),
                         pipeline_mode=pl.Buffered(1))]


W_OUT_SCRATCH = [pltpu.VMEM((D_MIX_ABC, D_MODEL), BF16), pltpu.VMEM((GROUP_W, D_MODEL), BF16)]


def _layer(batch, seq, l, x2, prev, g, w_in, w_out, prm):
    tiles = seq // LAYER_ROWS
    n = batch * seq
    has_residual = prev is not None

    def rows(width):
        return pl.BlockSpec((LAYER_ROWS, width), lambda b, t: (b * tiles + t, 0))

    def full(a):
        return pl.BlockSpec(a.shape, lambda b, t: (0,) * a.ndim)

    args = [x2] + (list(prev) + [w_out, w_out] if has_residual else []) + [g, w_in] + list(prm)
    in_specs = [rows(D_MODEL)]
    if has_residual:
        in_specs += [rows(D_MIX_ABC), rows(GROUP_W)] + _w_out_specs(w_out, l - 1)
    in_specs += [full(g), _resident(w_in, l)] + [full(a) for a in prm]
    out_specs = [rows(D_ATTN), rows(D_MIX_ABC)]
    out_shape = [jax.ShapeDtypeStruct((n, D_ATTN), F32),
                 jax.ShapeDtypeStruct((n, D_MIX_ABC), BF16)]
    if has_residual:
        out_specs = [rows(D_MODEL)] + out_specs
        out_shape = [jax.ShapeDtypeStruct((n, D_MODEL), F32)] + out_shape
    return pl.pallas_call(
        functools.partial(_layer_kernel, has_residual),
        grid=(batch, tiles),
        in_specs=in_specs,
        out_specs=out_specs,
        out_shape=out_shape,
        scratch_shapes=[
            pltpu.VMEM((CONV_PAD, GROUP_W), F32),
            pltpu.VMEM((CONV_PAD, GROUP_W), F32),
            pltpu.VMEM((SUBLANES, GROUP_W), F32),
            pltpu.VMEM((N_CHUNKS - N_MIX_CHUNKS + 1, GROUP_W // LANES, LAYER_ROWS, LANES), F32),
            pltpu.VMEM((D_MODEL, D_IN), BF16),
        ] + (W_OUT_SCRATCH if has_residual else []),
        compiler_params=pltpu.CompilerParams(
            dimension_semantics=("arbitrary", "arbitrary"), vmem_limit_bytes=VMEM_LIMIT),
        name="layer_res" if has_residual else "layer_first",
    )(*args)


def _out_final_kernel(x_ref, yabc_ref, yd_ref, w1_f32, w2_f32, g_ref, o_ref,
                      perm_buf, w1_ref, w2_ref):
    @pl.when(pl.program_id(0) == 0)
    def _():
        w1_ref[...] = w1_f32[...].astype(BF16)
        w2_ref[...] = w2_f32[...].astype(BF16)

    o_ref[...] = _rms(_residual(x_ref, yabc_ref, yd_ref, w1_ref, w2_ref, perm_buf), g_ref[...])


def _out_final(x2, y_abc, y_d, w_out, l, g):
    n = x2.shape[0]
    rows = lambda width: pl.BlockSpec((FINAL_ROWS, width), lambda i: (i, 0))
    full = lambda a: pl.BlockSpec(a.shape, lambda i: (0,) * a.ndim)
    return pl.pallas_call(
        _out_final_kernel,
        grid=(n // FINAL_ROWS,),
        in_specs=[rows(D_MODEL), rows(D_MIX_ABC), rows(GROUP_W)] + _w_out_specs(w_out, l) + [full(g)],
        out_specs=rows(D_MODEL),
        out_shape=jax.ShapeDtypeStruct((n, D_MODEL), F32),
        scratch_shapes=[pltpu.VMEM((GROUP_W // LANES, FINAL_ROWS, LANES), F32)] + W_OUT_SCRATCH,
        compiler_params=pltpu.CompilerParams(
            dimension_semantics=("arbitrary",), vmem_limit_bytes=VMEM_LIMIT),
        name="out_final",
    )(x2, y_abc, y_d, w_out, w_out, g)


def _attn_kernel(q0, q1, kp0, kp1, kc0, kc1, vp0, vp1, vc0, vc1, g_ref, y_ref,
                 bias_ref, acc_ref, m_ref, l_ref):
    q_ref, kp_ref, kc_ref, vp_ref, vc_ref = (q0, q1), (kp0, kp1), (kc0, kc1), (vp0, vp1), (vc0, vc1)
    first_tile = (pl.program_id(1) == 0).astype(jnp.int32)
    per = SUBTILE // RESIDUES

    def ld(halves, chunks):
        return jnp.concatenate(
            [jnp.concatenate([h[c, :] for h in halves], axis=1) for c, _ in chunks], axis=0)

    def ld_state(ref, chunks):
        return ld((ref.at[0], ref.at[1]), chunks)

    def st_rows(store, chunks, val):
        start = 0
        for c, size in chunks:
            store(c, val[start:start + size, :])
            start += size

    def st_state(ref, chunks, val):
        def store(c, v):
            ref[0, c, :] = v[:, :LANES]
            ref[1, c, :] = v[:, LANES:]
        st_rows(store, chunks, val)

    @pl.when((pl.program_id(0) == 0) & (pl.program_id(1) == 0))
    def _():
        qi = lax.broadcasted_iota(jnp.int32, (ATTN_BLOCK, 2 * ATTN_BLOCK), 0)
        kj = lax.broadcasted_iota(jnp.int32, (ATTN_BLOCK, 2 * ATTN_BLOCK), 1)
        q_major = RESIDUES * (qi % (per // RESIDUES)) + qi // (per // RESIDUES)
        k_major = (RESIDUES * (kj % (2 * per // RESIDUES) - per // RESIDUES)
                   + kj // (2 * per // RESIDUES))
        for p, dil in enumerate(ATTN_DILATIONS):
            steps = q_major - k_major if dil == 1 else qi + ATTN_BLOCK - kj
            in_block = k_major >= 0 if dil == 1 else kj >= ATTN_BLOCK
            band = (steps >= 0) & (steps <= ATTN_BLOCK)
            for hh in range(HEADS):
                slope = 2.0 ** (-8.0 * (hh + 1) / HEADS)
                bias = -slope * (steps * dil).astype(F32)
                bias_ref[(p * HEADS + hh) * 2] = jnp.where(band, bias, MASKED)
                bias_ref[(p * HEADS + hh) * 2 + 1] = jnp.where(band & in_block, bias, MASKED)

    lower_head = _lower_head_mask(ATTN_BLOCK)

    def expand(cols):
        return jnp.concatenate(
            [jnp.where(lower_head, cols[2 * i], cols[2 * i + 1]) for i in range(HEADS // 2)], axis=1)

    def pick_heads(per_head):
        return jnp.concatenate(
            [jnp.where(lower_head, per_head[2 * i][:, i * LANES:(i + 1) * LANES],
                       per_head[2 * i + 1][:, i * LANES:(i + 1) * LANES])
             for i in range(HEADS // 2)], axis=1)

    def run_blocks(mode, specs):
        scores = []
        for p, first, rows, kk, vv in specs:
            q = ld(q_ref, rows)
            q_heads = jnp.concatenate([_only_head(q, hh) for hh in range(HEADS)],
                                      axis=0).astype(BF16)
            scores.append(lax.dot_general(q_heads, kk.astype(BF16), (((1,), (1,)), ((), ())),
                                          preferred_element_type=F32))
        soft = []
        for (p, first, *_), s_all in zip(specs, scores):
            probs, m_cols, l_cols = [], [], []
            for hh in range(HEADS):
                s = s_all[hh * ATTN_BLOCK:(hh + 1) * ATTN_BLOCK, :]
                s = s + bias_ref[(p * HEADS + hh) * 2 + first]
                m_h = jnp.max(s, axis=-1, keepdims=True)
                e = jnp.exp(s - m_h)
                probs.append(e.astype(BF16))
                m_cols.append(m_h)
                l_cols.append(jnp.sum(e, axis=-1, keepdims=True))
            soft.append((jnp.concatenate(probs, axis=0), m_cols, l_cols))
        for (p, first, rows, kk, vv), (p_all, m_cols, l_cols) in zip(specs, soft):
            o_all = jnp.dot(p_all, vv.astype(BF16), preferred_element_type=F32)
            acc = pick_heads([o_all[hh * ATTN_BLOCK:(hh + 1) * ATTN_BLOCK, :] for hh in range(HEADS)])
            m_b = expand(m_cols)
            l_b = expand(l_cols)
            if mode != "init":
                m_s, l_s, acc_s = ld_state(m_ref, rows), ld_state(l_ref, rows), ld_state(acc_ref, rows)
                m_n = jnp.maximum(m_s, m_b)
                w_s = jnp.exp(m_s - m_n)
                w_b = jnp.exp(m_b - m_n)
                acc = acc_s * w_s + acc * w_b
                l_b = l_s * w_s + l_b * w_b
                m_b = m_n
            if mode == "final":
                gate = jnp.concatenate([g_ref[c, :] for c, _ in rows], axis=0)
                out = (acc / l_b * jax.nn.silu(gate)).astype(y_ref.dtype)

                def store(c, v):
                    y_ref[c, :] = v
                st_rows(store, rows, out)
            else:
                st_state(m_ref, rows, m_b)
                st_state(l_ref, rows, l_b)
                st_state(acc_ref, rows, acc)

    def keys_values(prev_refs, prev_rows, rows):
        k_src, v_src = (kp_ref, vp_ref) if prev_refs else (kc_ref, vc_ref)
        kk = jnp.concatenate([ld(k_src, prev_rows), ld(kc_ref, rows)], axis=0)
        vv = jnp.concatenate([ld(v_src, prev_rows), ld(vc_ref, rows)], axis=0)
        return kk, vv


    def spec16(r):
        a, b = r // RESIDUES, r % RESIDUES
        rows = [(pl.ds(t * SUBTILE + b * per + a, per // RESIDUES, stride=RESIDUES), per // RESIDUES)
                for t in range(ATTN_TILE // SUBTILE)]
        return (2, first_tile, rows) + keys_values(1, rows, rows)

    def body16(i, carry):
        run_blocks("init", [spec16(GROUP_STRIDE16 * i + j) for j in range(GROUP_STRIDE16)])
        return carry
    lax.fori_loop(0, 16 // GROUP_STRIDE16, body16, 0)

    def spec4(t, b):
        rows = [(pl.ds(t * SUBTILE + b * per, per), per)]
        t_prev = (t - 1) % (ATTN_TILE // SUBTILE)
        prev_rows = [(pl.ds(t_prev * SUBTILE + b * per, per), per)]
        first = first_tile if t == 0 else 0
        return (1, first, rows) + keys_values(t == 0, prev_rows, rows)

    for t in range(ATTN_TILE // SUBTILE):
        run_blocks("merge", [spec4(t, b) for b in range(RESIDUES)])

    n_sub = SUBTILE // ATTN_BLOCK
    blk = per // n_sub

    def spec1(t, u, t_static):
        base = t * SUBTILE
        if not isinstance(base, int):
            base = pl.multiple_of(base, SUBTILE)
        rows = [(pl.ds(base + b * per + u * blk, blk), blk) for b in range(RESIDUES)]
        if u > 0:
            key_rows = [(pl.ds(base + b * per + (u - 1) * blk, 2 * blk), 2 * blk)
                        for b in range(RESIDUES)]
            return (0, 0, rows, ld(kc_ref, key_rows), ld(vc_ref, key_rows))
        first_of_tile = t_static == 0
        k_src, v_src = (kp_ref, vp_ref) if first_of_tile else (kc_ref, vc_ref)
        prev_base = (ATTN_TILE - SUBTILE) if first_of_tile else base - SUBTILE
        kk, vv = [], []
        for b in range(RESIDUES):
            before = [(pl.ds(prev_base + b * per + per - blk, blk), blk)]
            kk += [ld(k_src, before), ld(kc_ref, rows[b:b + 1])]
            vv += [ld(v_src, before), ld(vc_ref, rows[b:b + 1])]
        first = first_tile if first_of_tile else 0
        return (0, first, rows, jnp.concatenate(kk, axis=0), jnp.concatenate(vv, axis=0))

    assert GROUP_CONTIGUOUS == n_sub
    run_blocks("final", [spec1(0, u, 0) for u in range(n_sub)])

    def body1(t, carry):
        run_blocks("final", [spec1(t, u, None) for u in range(n_sub)])
        return carry
    lax.fori_loop(1, ATTN_TILE // SUBTILE, body1, 0)


def _attention(zq, batch, seq):
    tiles = seq // ATTN_TILE
    halves = GROUP_W // LANES

    def cur(c):
        return [pl.BlockSpec((ATTN_TILE, LANES), lambda b, n, c=c, h=h: (b * tiles + n, c * halves + h))
                for h in range(halves)]

    def prev(c):
        return [pl.BlockSpec((ATTN_TILE, LANES),
                             lambda b, n, c=c, h=h: (b * tiles + jnp.maximum(n - 1, 0), c * halves + h))
                for h in range(halves)]

    in_specs = cur(0) + prev(1) + cur(1) + prev(2) + cur(2) + [
        pl.BlockSpec((ATTN_TILE, GROUP_W), lambda b, n: (b * tiles + n, 3))]
    state = pltpu.VMEM((halves, ATTN_TILE, LANES), F32)
    return pl.pallas_call(
        _attn_kernel,
        grid=(batch, tiles),
        in_specs=in_specs,
        out_specs=pl.BlockSpec((ATTN_TILE, GROUP_W), lambda b, n: (b * tiles + n, 0)),
        out_shape=jax.ShapeDtypeStruct((batch * seq, GROUP_W), BF16),
        scratch_shapes=[
            pltpu.VMEM((len(ATTN_DILATIONS) * HEADS * 2, ATTN_BLOCK, 2 * ATTN_BLOCK), F32),
            state, state, state,
        ],
        compiler_params=pltpu.CompilerParams(
            dimension_semantics=("arbitrary", "arbitrary"), vmem_limit_bytes=VMEM_LIMIT),
        name="dilated_attention",
    )(*([zq] * len(in_specs)))


def _block_diag(w):
    h, d, _ = w.shape
    eye = jnp.eye(h, dtype=w.dtype)
    return jnp.einsum("hij,hg->higj", w, eye).reshape(h * d, h * d)


def kernel(x, norm_g, w_in, conv_a_w, conv_r_w, conv_r_b, lru_wa, lru_ba, lru_wx, lru_bx,
           lru_lambda, gmlp_norm_g, gmlp_ws, gmlp_bs, w_out, final_g):
    batch, seq, _ = x.shape
    depth = w_in.shape[0]
    x2 = x.reshape(batch * seq, D_MODEL)
    row = lambda v: v.reshape(1, -1)
    prev = None
    for l in range(depth):
        prm = [
            conv_a_w[l], conv_r_w[l], row(conv_r_b[l]),
            _block_diag(lru_wa[l]).astype(BF16), row(lru_ba[l]),
            _block_diag(lru_wx[l]).astype(BF16), row(lru_bx[l]),
            row(lru_lambda[l]), row(gmlp_norm_g[l]),
            jnp.transpose(gmlp_ws[l], (1, 0, 2)).reshape(GMLP_CHUNK, HEADS * GMLP_CHUNK),
            jnp.repeat(jnp.transpose(gmlp_bs[l]), HEAD_DIM, axis=1),
        ]
        outs = _layer(batch, seq, l, x2, prev, row(norm_g[l]), w_in, w_out, prm)
        if prev is not None:
            x2 = outs[0]
        zq, y_abc = outs[-2:]
        prev = (y_abc, _attention(zq, batch, seq))
    out = _out_final(x2, *prev, w_out, depth - 1, row(final_g))
    return out.reshape(batch, seq, D_MODEL)
```

```python
import functools
import math

import jax
import jax.numpy as jnp
from jax import lax
from jax.experimental import pallas as pl
from jax.experimental.pallas import tpu as pltpu

F32 = jnp.float32
BF16 = jnp.bfloat16

D_MODEL = 1024
GROUP_W = 256
HEADS = 4
HEAD_DIM = 64
N_CHUNKS = 13
N_MIX_CHUNKS = 9
D_IN = N_CHUNKS * GROUP_W
D_MIX_ABC = 3 * GROUP_W
D_ATTN = (N_CHUNKS - N_MIX_CHUNKS) * GROUP_W
RG_C = 8.0
GMLP_CHUNK = 128
ATTN_BLOCK = 128
LANES = 128
ATTN_DILATIONS = (1, 4, 16)
ATTN_TILE = ATTN_BLOCK * 16
GROUP_STRIDE16 = 8
SUBTILES_PER_GROUP = 2
SCORE_LOOKAHEAD = 1
NORM_EPS = 1e-6
MASKED = -1e30
LOG2_E = math.log2(math.e)
Q_SCALE = LOG2_E / math.sqrt(HEAD_DIM)

LAYER_ROWS = 512
RESIDUES = 4
SUBTILE = LAYER_ROWS
assert ATTN_TILE % SUBTILE == 0 and SUBTILE == 4 * ATTN_BLOCK
FINAL_ROWS = 2 * SUBTILE
SUBLANES = 8
CONV_PAD = SUBLANES
VMEM_LIMIT = 56 * 1024 * 1024


def _rms(x, g):
    return x * lax.rsqrt(jnp.mean(x * x, axis=-1, keepdims=True) + NORM_EPS) * g


def _lower_head_mask(rows):
    return lax.broadcasted_iota(jnp.int32, (rows, LANES), 1) < HEAD_DIM


def _only_head(x, hh):
    lower_head = _lower_head_mask(x.shape[0])
    half, lower = hh // 2, hh % 2 == 0
    part = x[:, half * LANES:(half + 1) * LANES]
    part = jnp.where(lower_head, part, 0.0) if lower else jnp.where(lower_head, 0.0, part)
    zeros = jnp.zeros_like(part)
    return jnp.concatenate([part if i == half else zeros for i in range(HEADS // 2)], axis=1)


def _to_residue_major(buf, x):
    per = x.shape[0] // RESIDUES
    for half in range(GROUP_W // LANES):
        buf[half] = x[:, half * LANES:(half + 1) * LANES]
    return jnp.concatenate(
        [jnp.concatenate([buf[half, pl.ds(b, per, stride=RESIDUES), :]
                          for half in range(GROUP_W // LANES)], axis=1)
         for b in range(RESIDUES)], axis=0)


def _from_residue_major(buf, x):
    per = SUBTILE // RESIDUES
    for start in range(0, x.shape[0], SUBTILE):
        for b in range(RESIDUES):
            for half in range(GROUP_W // LANES):
                buf[half, pl.ds(start + b, per, stride=RESIDUES), :] = (
                    x[start + b * per:start + (b + 1) * per, half * LANES:(half + 1) * LANES])
    return jnp.concatenate([buf[half] for half in range(GROUP_W // LANES)], axis=1)


def _causal_conv(carry_ref, x, w_ref):
    rows = x.shape[0]
    taps = w_ref.shape[0]
    history = carry_ref[...]
    carry_ref[...] = x[rows - CONV_PAD:, :]
    row = lax.broadcasted_iota(jnp.int32, history.shape, 0)
    out = x * w_ref[taps - 1:taps, :]
    for k in range(taps - 1):
        shift = taps - 1 - k
        rolled = pltpu.roll(x, shift, axis=0)
        head = jnp.where(row < shift, pltpu.roll(history, shift, axis=0), rolled[0:CONV_PAD])
        shifted = jnp.concatenate([head, rolled[CONV_PAD:]], axis=0)
        out += shifted * w_ref[k:k + 1, :]
    return out


def _linear_scan(a, b, h0):
    rows, cols = a.shape
    groups = rows // SUBLANES
    a = a.reshape(groups, SUBLANES, cols)
    b = b.reshape(groups, SUBLANES, cols)
    row = lax.broadcasted_iota(jnp.int32, a.shape, 1)
    d = 1
    while d < SUBLANES:
        keep = row >= d
        a_prev = jnp.where(keep, pltpu.roll(a, d, axis=1), 1.0)
        b_prev = jnp.where(keep, pltpu.roll(b, d, axis=1), 0.0)
        b = a * b_prev + b
        a = a * a_prev
        d *= 2
    out = []
    for g in range(groups):
        hg = b[g] + a[g] * h0
        out.append(hg)
        h0 = hg[SUBLANES - 1:SUBLANES, :]
    return jnp.concatenate(out, axis=0)


def _mixers_abc(z, emit_other_matmuls, prm, y_ref, carry_a, carry_r, carry_h):
    (conv_a_w_ref, conv_r_w_ref, conv_r_b_ref, wa_ref, ba_ref, wx_ref, bx_ref,
     lam_ref, gmlp_g_ref, ws_ref, bs_ref) = prm
    r_x, c_v = z(4), z(7)
    a_x, a_b, a_c, a_g = z(0), z(1), z(2), z(3)
    r_g, c_u, c_g = z(5), z(6), z(8)
    rows = a_x.shape[0]

    conv_a = _causal_conv(carry_a, a_c * a_x, conv_a_w_ref)
    y_ref[:, 0:GROUP_W] = (a_b * conv_a * jax.nn.silu(a_g)).astype(y_ref.dtype)

    xb = _causal_conv(carry_r, r_x, conv_r_w_ref) + conv_r_b_ref[...]
    xb16 = xb.astype(BF16)
    r = jax.nn.sigmoid(jnp.dot(xb16, wa_ref[...], preferred_element_type=F32) + ba_ref[...])
    i = jax.nn.sigmoid(jnp.dot(xb16, wx_ref[...], preferred_element_type=F32) + bx_ref[...])
    log_a = (-RG_C * r) * jax.nn.softplus(-lam_ref[...])
    a = jnp.exp(log_a)
    th = jnp.tanh(log_a)
    mult = jnp.sqrt(-2.0 * th / (1.0 - th))
    h = _linear_scan(a, mult * (i * xb), carry_h[0:1, :])
    carry_h[...] = jnp.broadcast_to(h[rows - 1:rows, :], carry_h.shape)
    y_ref[:, GROUP_W:2 * GROUP_W] = (h * jax.nn.silu(r_g)).astype(y_ref.dtype)

    vv = _rms(jax.nn.gelu(c_v), gmlp_g_ref[...])
    t_idx = lax.broadcasted_iota(jnp.int32, ws_ref.shape, 0)
    s_idx = lax.broadcasted_iota(jnp.int32, ws_ref.shape, 1) % GMLP_CHUNK
    ws = jnp.where(s_idx <= t_idx, ws_ref[...], 0.0).astype(BF16)
    emit_other_matmuls()
    for j in range(rows // GMLP_CHUNK):
        sl = slice(j * GMLP_CHUNK, (j + 1) * GMLP_CHUNK)
        vj = vv[sl, :]
        v_heads = jnp.concatenate([_only_head(vj, hh) for hh in range(HEADS)], axis=0).astype(BF16)
        sp = jnp.dot(ws, v_heads, preferred_element_type=F32) + bs_ref[...]
        yc = jax.nn.gelu(c_u[sl, :]) * sp * jax.nn.silu(c_g[sl, :])
        y_ref[sl, 2 * GROUP_W:3 * GROUP_W] = yc.astype(y_ref.dtype)


N_MIX_PARAMS = 11


def _residual(x_ref, yabc_ref, yd_ref, w1_ref, w2_ref, perm_buf):
    y_d = _from_residue_major(perm_buf, yd_ref[...].astype(F32)).astype(BF16)
    acc = jnp.dot(yabc_ref[...], w1_ref[...], preferred_element_type=F32)
    acc += jnp.dot(y_d, w2_ref[...], preferred_element_type=F32)
    return x_ref[...] + acc


def _layer_kernel(has_residual, *refs):
    refs = list(refs)
    take = lambda n: [refs.pop(0) for _ in range(n)]
    if has_residual:
        x_ref, yabc_ref, yd_ref, w1_f32, w2_f32 = take(5)
    else:
        (x_ref,) = take(1)
    g_ref, w_f32 = take(2)
    prm = take(N_MIX_PARAMS)
    if has_residual:
        (xo_ref,) = take(1)
    zq_ref, y_ref, carry_a, carry_r, carry_h, perm_buf, w_ref = refs[:7]
    if has_residual:
        w1_ref, w2_ref = refs[7:]

    @pl.when((pl.program_id(0) == 0) & (pl.program_id(1) == 0))
    def _():
        w_ref[...] = w_f32[...].astype(BF16)
        if has_residual:
            w1_ref[...] = w1_f32[...].astype(BF16)
            w2_ref[...] = w2_f32[...].astype(BF16)

    @pl.when(pl.program_id(1) == 0)
    def _():
        carry_a[...] = jnp.zeros_like(carry_a)
        carry_r[...] = jnp.zeros_like(carry_r)
        carry_h[...] = jnp.zeros_like(carry_h)

    if has_residual:
        x = _residual(x_ref, yabc_ref, yd_ref, w1_ref, w2_ref, perm_buf.at[N_CHUNKS - N_MIX_CHUNKS])
        xo_ref[...] = x
    else:
        x = x_ref[...]
    h = _rms(x, g_ref[...]).astype(BF16)

    def z(c):
        cols = slice(c * GROUP_W, (c + 1) * GROUP_W)
        return jnp.dot(h, w_ref[:, cols], preferred_element_type=F32)

    def attention_chunks():
        for k, c in enumerate(range(N_MIX_CHUNKS, N_CHUNKS)):
            zc = z(c) * Q_SCALE if k == 0 else z(c)
            zq_ref[:, k * GROUP_W:(k + 1) * GROUP_W] = _to_residue_major(perm_buf.at[k], zc)

    _mixers_abc(z, attention_chunks, prm, y_ref, carry_a, carry_r, carry_h)


def _resident(a, l):
    return pl.BlockSpec((None,) + a.shape[1:], lambda *_: (l,) + (0,) * (a.ndim - 1),
                        pipeline_mode=pl.Buffered(1))


def _w_out_specs(w_out, l):
    assert D_MIX_ABC % GROUP_W == 0
    return [pl.BlockSpec((None, D_MIX_ABC, D_MODEL), lambda *_: (l, 0, 0),
                         pipeline_mode=pl.Buffered(1)),
            pl.BlockSpec((None, GROUP_W, D_MODEL), lambda *_: (l, D_MIX_ABC // GROUP_W, 0),
                         pipeline_mode=pl.Buffered(1))]


W_OUT_SCRATCH = [pltpu.VMEM((D_MIX_ABC, D_MODEL), BF16), pltpu.VMEM((GROUP_W, D_MODEL), BF16)]


def _layer(batch, seq, l, x2, prev, g, w_in, w_out, prm):
    tiles = seq // LAYER_ROWS
    n = batch * seq
    has_residual = prev is not None

    def rows(width):
        return pl.BlockSpec((LAYER_ROWS, width), lambda b, t: (b * tiles + t, 0))

    def full(a):
        return pl.BlockSpec(a.shape, lambda b, t: (0,) * a.ndim)

    args = [x2] + (list(prev) + [w_out, w_out] if has_residual else []) + [g, w_in] + list(prm)
    in_specs = [rows(D_MODEL)]
    if has_residual:
        in_specs += [rows(D_MIX_ABC), rows(GROUP_W)] + _w_out_specs(w_out, l - 1)
    in_specs += [full(g), _resident(w_in, l)] + [full(a) for a in prm]
    out_specs = [rows(D_ATTN), rows(D_MIX_ABC)]
    out_shape = [jax.ShapeDtypeStruct((n, D_ATTN), F32),
                 jax.ShapeDtypeStruct((n, D_MIX_ABC), BF16)]
    if has_residual:
        out_specs = [rows(D_MODEL)] + out_specs
        out_shape = [jax.ShapeDtypeStruct((n, D_MODEL), F32)] + out_shape
    return pl.pallas_call(
        functools.partial(_layer_kernel, has_residual),
        grid=(batch, tiles),
        in_specs=in_specs,
        out_specs=out_specs,
        out_shape=out_shape,
        scratch_shapes=[
            pltpu.VMEM((CONV_PAD, GROUP_W), F32),
            pltpu.VMEM((CONV_PAD, GROUP_W), F32),
            pltpu.VMEM((SUBLANES, GROUP_W), F32),
            pltpu.VMEM((N_CHUNKS - N_MIX_CHUNKS + 1, GROUP_W // LANES, LAYER_ROWS, LANES), F32),
            pltpu.VMEM((D_MODEL, D_IN), BF16),
        ] + (W_OUT_SCRATCH if has_residual else []),
        compiler_params=pltpu.CompilerParams(
            dimension_semantics=("arbitrary", "arbitrary"), vmem_limit_bytes=VMEM_LIMIT),
        name="layer_res" if has_residual else "layer_first",
    )(*args)


def _out_final_kernel(x_ref, yabc_ref, yd_ref, w1_f32, w2_f32, g_ref, o_ref,
                      perm_buf, w1_ref, w2_ref):
    @pl.when(pl.program_id(0) == 0)
    def _():
        w1_ref[...] = w1_f32[...].astype(BF16)
        w2_ref[...] = w2_f32[...].astype(BF16)

    o_ref[...] = _rms(_residual(x_ref, yabc_ref, yd_ref, w1_ref, w2_ref, perm_buf), g_ref[...])


def _out_final(x2, y_abc, y_d, w_out, l, g):
    n = x2.shape[0]
    rows = lambda width: pl.BlockSpec((FINAL_ROWS, width), lambda i: (i, 0))
    full = lambda a: pl.BlockSpec(a.shape, lambda i: (0,) * a.ndim)
    return pl.pallas_call(
        _out_final_kernel,
        grid=(n // FINAL_ROWS,),
        in_specs=[rows(D_MODEL), rows(D_MIX_ABC), rows(GROUP_W)] + _w_out_specs(w_out, l) + [full(g)],
        out_specs=rows(D_MODEL),
        out_shape=jax.ShapeDtypeStruct((n, D_MODEL), F32),
        scratch_shapes=[pltpu.VMEM((GROUP_W // LANES, FINAL_ROWS, LANES), F32)] + W_OUT_SCRATCH,
        compiler_params=pltpu.CompilerParams(
            dimension_semantics=("arbitrary",), vmem_limit_bytes=VMEM_LIMIT),
        name="out_final",
    )(x2, y_abc, y_d, w_out, w_out, g)


def _attn_kernel(q0, q1, kp0, kp1, kc0, kc1, vp0, vp1, vc0, vc1, g_ref, y_ref,
                 bias_ref, acc_ref, m_ref, l_ref):
    q_ref, kp_ref, kc_ref, vp_ref, vc_ref = (q0, q1), (kp0, kp1), (kc0, kc1), (vp0, vp1), (vc0, vc1)
    first_tile = (pl.program_id(1) == 0).astype(jnp.int32)
    per = SUBTILE // RESIDUES

    def ld(halves, chunks):
        return jnp.concatenate(
            [jnp.concatenate([h[c, :] for h in halves], axis=1) for c, _ in chunks], axis=0)

    def ld_state(ref, chunks):
        return ld((ref.at[0], ref.at[1]), chunks)

    def st_rows(store, chunks, val):
        start = 0
        for c, size in chunks:
            store(c, val[start:start + size, :])
            start += size

    def st_state(ref, chunks, val):
        def store(c, v):
            ref[0, c, :] = v[:, :LANES]
            ref[1, c, :] = v[:, LANES:]
        st_rows(store, chunks, val)

    @pl.when((pl.program_id(0) == 0) & (pl.program_id(1) == 0))
    def _():
        qi = lax.broadcasted_iota(jnp.int32, (ATTN_BLOCK, 2 * ATTN_BLOCK), 0)
        kj = lax.broadcasted_iota(jnp.int32, (ATTN_BLOCK, 2 * ATTN_BLOCK), 1)
        q_major = RESIDUES * (qi % (per // RESIDUES)) + qi // (per // RESIDUES)
        k_major = (RESIDUES * (kj % (2 * per // RESIDUES) - per // RESIDUES)
                   + kj // (2 * per // RESIDUES))
        for p, dil in enumerate(ATTN_DILATIONS):
            steps = q_major - k_major if dil == 1 else qi + ATTN_BLOCK - kj
            in_block = k_major >= 0 if dil == 1 else kj >= ATTN_BLOCK
            band = (steps >= 0) & (steps <= ATTN_BLOCK)
            for hh in range(HEADS):
                slope = 2.0 ** (-8.0 * (hh + 1) / HEADS)
                bias = (-slope * LOG2_E) * (steps * dil).astype(F32)
                bias_ref[(p * HEADS + hh) * 2] = jnp.where(band, bias, MASKED)
                bias_ref[(p * HEADS + hh) * 2 + 1] = jnp.where(band & in_block, bias, MASKED)

    lower_head = _lower_head_mask(ATTN_BLOCK)

    def expand(cols):
        return jnp.concatenate(
            [jnp.where(lower_head, cols[2 * i], cols[2 * i + 1]) for i in range(HEADS // 2)], axis=1)

    def pick_heads(per_head):
        return jnp.concatenate(
            [jnp.where(lower_head, per_head[2 * i][:, i * LANES:(i + 1) * LANES],
                       per_head[2 * i + 1][:, i * LANES:(i + 1) * LANES])
             for i in range(HEADS // 2)], axis=1)

    def run_blocks(mode, specs):
        def scores_of(spec):
            p, first, rows, kk, vv = spec
            q = ld(q_ref, rows)
            q_heads = jnp.concatenate([_only_head(q, hh) for hh in range(HEADS)],
                                      axis=0).astype(BF16)
            return lax.dot_general(q_heads, kk.astype(BF16), (((1,), (1,)), ((), ())),
                                   preferred_element_type=F32)

        def softmax_of(spec, s_all):
            p, first = spec[:2]
            probs, m_cols, l_cols = [], [], []
            for hh in range(HEADS):
                s = s_all[hh * ATTN_BLOCK:(hh + 1) * ATTN_BLOCK, :]
                s = s + bias_ref[(p * HEADS + hh) * 2 + first]
                m_h = jnp.max(s, axis=-1, keepdims=True)
                e = jnp.exp2(s - m_h)
                probs.append(e.astype(BF16))
                m_cols.append(m_h)
                l_cols.append(jnp.sum(e, axis=-1, keepdims=True))
            return jnp.concatenate(probs, axis=0), m_cols, l_cols

        def values_of(spec, p_all, m_cols, l_cols):
            p, first, rows, kk, vv = spec
            o_all = jnp.dot(p_all, vv.astype(BF16), preferred_element_type=F32)
            acc = pick_heads([o_all[hh * ATTN_BLOCK:(hh + 1) * ATTN_BLOCK, :] for hh in range(HEADS)])
            m_b = expand(m_cols)
            l_b = expand(l_cols)
            if mode != "init":
                m_s, l_s, acc_s = ld_state(m_ref, rows), ld_state(l_ref, rows), ld_state(acc_ref, rows)
                m_n = jnp.maximum(m_s, m_b)
                w_s = jnp.exp2(m_s - m_n)
                w_b = jnp.exp2(m_b - m_n)
                acc = acc_s * w_s + acc * w_b
                l_b = l_s * w_s + l_b * w_b
                m_b = m_n
            if mode == "final":
                gate = jnp.concatenate([g_ref[c, :] for c, _ in rows], axis=0)
                out = (acc / l_b * jax.nn.silu(gate)).astype(y_ref.dtype)

                def store(c, v):
                    y_ref[c, :] = v
                st_rows(store, rows, out)
            else:
                st_state(m_ref, rows, m_b)
                st_state(l_ref, rows, l_b)
                st_state(acc_ref, rows, acc)

        scores = {i: scores_of(specs[i]) for i in range(min(SCORE_LOOKAHEAD, len(specs)))}
        for i, spec in enumerate(specs):
            soft = softmax_of(spec, scores.pop(i))
            if i + SCORE_LOOKAHEAD < len(specs):
                scores[i + SCORE_LOOKAHEAD] = scores_of(specs[i + SCORE_LOOKAHEAD])
            values_of(spec, *soft)

    def keys_values(prev_refs, prev_rows, rows):
        k_src, v_src = (kp_ref, vp_ref) if prev_refs else (kc_ref, vc_ref)
        kk = jnp.concatenate([ld(k_src, prev_rows), ld(kc_ref, rows)], axis=0)
        vv = jnp.concatenate([ld(v_src, prev_rows), ld(vc_ref, rows)], axis=0)
        return kk, vv


    def spec16(r):
        a, b = r // RESIDUES, r % RESIDUES
        rows = [(pl.ds(t * SUBTILE + b * per + a, per // RESIDUES, stride=RESIDUES), per // RESIDUES)
                for t in range(ATTN_TILE // SUBTILE)]
        return (2, first_tile, rows) + keys_values(1, rows, rows)

    def body16(i, carry):
        run_blocks("init", [spec16(GROUP_STRIDE16 * i + j) for j in range(GROUP_STRIDE16)])
        return carry
    lax.fori_loop(0, 16 // GROUP_STRIDE16, body16, 0)

    def spec4(t, b):
        rows = [(pl.ds(t * SUBTILE + b * per, per), per)]
        t_prev = (t - 1) % (ATTN_TILE // SUBTILE)
        prev_rows = [(pl.ds(t_prev * SUBTILE + b * per, per), per)]
        first = first_tile if t == 0 else 0
        return (1, first, rows) + keys_values(t == 0, prev_rows, rows)

    for t0 in range(0, ATTN_TILE // SUBTILE, SUBTILES_PER_GROUP):
        run_blocks("merge", [spec4(t, b) for t in range(t0, t0 + SUBTILES_PER_GROUP)
                             for b in range(RESIDUES)])

    n_sub = SUBTILE // ATTN_BLOCK
    blk = per // n_sub

    def spec1(t, u):
        base = t * SUBTILE
        rows = [(pl.ds(base + b * per + u * blk, blk), blk) for b in range(RESIDUES)]
        if u > 0:
            key_rows = [(pl.ds(base + b * per + (u - 1) * blk, 2 * blk), 2 * blk)
                        for b in range(RESIDUES)]
            return (0, 0, rows, ld(kc_ref, key_rows), ld(vc_ref, key_rows))
        first_of_tile = t == 0
        k_src, v_src = (kp_ref, vp_ref) if first_of_tile else (kc_ref, vc_ref)
        prev_base = (ATTN_TILE - SUBTILE) if first_of_tile else base - SUBTILE
        kk, vv = [], []
        for b in range(RESIDUES):
            before = [(pl.ds(prev_base + b * per + per - blk, blk), blk)]
            kk += [ld(k_src, before), ld(kc_ref, rows[b:b + 1])]
            vv += [ld(v_src, before), ld(vc_ref, rows[b:b + 1])]
        first = first_tile if first_of_tile else 0
        return (0, first, rows, jnp.concatenate(kk, axis=0), jnp.concatenate(vv, axis=0))

    for t0 in range(0, ATTN_TILE // SUBTILE, SUBTILES_PER_GROUP):
        run_blocks("final", [spec1(t, u) for t in range(t0, t0 + SUBTILES_PER_GROUP)
                             for u in range(n_sub)])


def _attention(zq, batch, seq):
    tiles = seq // ATTN_TILE
    halves = GROUP_W // LANES

    def cur(c):
        return [pl.BlockSpec((ATTN_TILE, LANES), lambda b, n, c=c, h=h: (b * tiles + n, c * halves + h))
                for h in range(halves)]

    def prev(c):
        return [pl.BlockSpec((ATTN_TILE, LANES),
                             lambda b, n, c=c, h=h: (b * tiles + jnp.maximum(n - 1, 0), c * halves + h))
                for h in range(halves)]

    in_specs = cur(0) + prev(1) + cur(1) + prev(2) + cur(2) + [
        pl.BlockSpec((ATTN_TILE, GROUP_W), lambda b, n: (b * tiles + n, 3))]
    state = pltpu.VMEM((halves, ATTN_TILE, LANES), F32)
    return pl.pallas_call(
        _attn_kernel,
        grid=(batch, tiles),
        in_specs=in_specs,
        out_specs=pl.BlockSpec((ATTN_TILE, GROUP_W), lambda b, n: (b * tiles + n, 0)),
        out_shape=jax.ShapeDtypeStruct((batch * seq, GROUP_W), BF16),
        scratch_shapes=[
            pltpu.VMEM((len(ATTN_DILATIONS) * HEADS * 2, ATTN_BLOCK, 2 * ATTN_BLOCK), F32),
            state, state, state,
        ],
        compiler_params=pltpu.CompilerParams(
            dimension_semantics=("arbitrary", "arbitrary"), vmem_limit_bytes=VMEM_LIMIT),
        name="dilated_attention",
    )(*([zq] * len(in_specs)))


def _block_diag(w):
    h, d, _ = w.shape
    eye = jnp.eye(h, dtype=w.dtype)
    return jnp.einsum("hij,hg->higj", w, eye).reshape(h * d, h * d)


def kernel(x, norm_g, w_in, conv_a_w, conv_r_w, conv_r_b, lru_wa, lru_ba, lru_wx, lru_bx,
           lru_lambda, gmlp_norm_g, gmlp_ws, gmlp_bs, w_out, final_g):
    batch, seq, _ = x.shape
    depth = w_in.shape[0]
    x2 = x.reshape(batch * seq, D_MODEL)
    row = lambda v: v.reshape(1, -1)
    prev = None
    for l in range(depth):
        prm = [
            conv_a_w[l], conv_r_w[l], row(conv_r_b[l]),
            _block_diag(lru_wa[l]).astype(BF16), row(lru_ba[l]),
            _block_diag(lru_wx[l]).astype(BF16), row(lru_bx[l]),
            row(lru_lambda[l]), row(gmlp_norm_g[l]),
            jnp.transpose(gmlp_ws[l], (1, 0, 2)).reshape(GMLP_CHUNK, HEADS * GMLP_CHUNK),
            jnp.repeat(jnp.transpose(gmlp_bs[l]), HEAD_DIM, axis=1),
        ]
        outs = _layer(batch, seq, l, x2, prev, row(norm_g[l]), w_in, w_out, prm)
        if prev is not None:
            x2 = outs[0]
        zq, y_abc = outs[-2:]
        prev = (y_abc, _attention(zq, batch, seq))
    out = _out_final(x2, *prev, w_out, depth - 1, row(final_g))
    return out.reshape(batch, seq, D_MODEL)
```

```python
import functools
import math

import jax
import jax.numpy as jnp
from jax import lax
from jax.experimental import pallas as pl
from jax.experimental.pallas import tpu as pltpu

F32 = jnp.float32
BF16 = jnp.bfloat16

D_MODEL = 1024
GROUP_W = 256
HEADS = 4
HEAD_DIM = 64
N_CHUNKS = 13
N_MIX_CHUNKS = 9
D_IN = N_CHUNKS * GROUP_W
D_MIX_ABC = 3 * GROUP_W
D_ATTN = (N_CHUNKS - N_MIX_CHUNKS) * GROUP_W
RG_C = 8.0
GMLP_CHUNK = 128
ATTN_BLOCK = 128
LANES = 128
ATTN_DILATIONS = (1, 4, 16)
ATTN_TILE = ATTN_BLOCK * 16
GROUP_STRIDE16 = 16
SUBTILES_PER_GROUP = 4
SCORE_LOOKAHEAD = 1
NORM_EPS = 1e-6
MASKED = -1e30
LOG2_E = math.log2(math.e)
Q_SCALE = LOG2_E / math.sqrt(HEAD_DIM)

LAYER_ROWS = 512
RESIDUES = 4
SUBTILE = LAYER_ROWS
assert ATTN_TILE % SUBTILE == 0 and SUBTILE == 4 * ATTN_BLOCK
FINAL_ROWS = 2 * SUBTILE
SUBLANES = 8
CONV_PAD = SUBLANES
VMEM_LIMIT = 56 * 1024 * 1024


def _rms(x, g):
    return x * lax.rsqrt(jnp.mean(x * x, axis=-1, keepdims=True) + NORM_EPS) * g


def _lower_head_mask(rows):
    return lax.broadcasted_iota(jnp.int32, (rows, LANES), 1) < HEAD_DIM


def _only_head(x, hh):
    lower_head = _lower_head_mask(x.shape[0])
    half, lower = hh // 2, hh % 2 == 0
    part = x[:, half * LANES:(half + 1) * LANES]
    part = jnp.where(lower_head, part, 0.0) if lower else jnp.where(lower_head, 0.0, part)
    zeros = jnp.zeros_like(part)
    return jnp.concatenate([part if i == half else zeros for i in range(HEADS // 2)], axis=1)


def _to_residue_major(buf, x):
    per = x.shape[0] // RESIDUES
    for half in range(GROUP_W // LANES):
        buf[half] = x[:, half * LANES:(half + 1) * LANES]
    return jnp.concatenate(
        [jnp.concatenate([buf[half, pl.ds(b, per, stride=RESIDUES), :]
                          for half in range(GROUP_W // LANES)], axis=1)
         for b in range(RESIDUES)], axis=0)


def _from_residue_major(buf, x):
    per = SUBTILE // RESIDUES
    for start in range(0, x.shape[0], SUBTILE):
        for b in range(RESIDUES):
            for half in range(GROUP_W // LANES):
                buf[half, pl.ds(start + b, per, stride=RESIDUES), :] = (
                    x[start + b * per:start + (b + 1) * per, half * LANES:(half + 1) * LANES])
    return jnp.concatenate([buf[half] for half in range(GROUP_W // LANES)], axis=1)


def _causal_conv(carry_ref, x, w_ref):
    rows = x.shape[0]
    taps = w_ref.shape[0]
    history = carry_ref[...]
    carry_ref[...] = x[rows - CONV_PAD:, :]
    row = lax.broadcasted_iota(jnp.int32, history.shape, 0)
    out = x * w_ref[taps - 1:taps, :]
    for k in range(taps - 1):
        shift = taps - 1 - k
        rolled = pltpu.roll(x, shift, axis=0)
        head = jnp.where(row < shift, pltpu.roll(history, shift, axis=0), rolled[0:CONV_PAD])
        shifted = jnp.concatenate([head, rolled[CONV_PAD:]], axis=0)
        out += shifted * w_ref[k:k + 1, :]
    return out


def _linear_scan(a, b, h0):
    rows, cols = a.shape
    groups = rows // SUBLANES
    a = a.reshape(groups, SUBLANES, cols)
    b = b.reshape(groups, SUBLANES, cols)
    row = lax.broadcasted_iota(jnp.int32, a.shape, 1)
    d = 1
    while d < SUBLANES:
        keep = row >= d
        a_prev = jnp.where(keep, pltpu.roll(a, d, axis=1), 1.0)
        b_prev = jnp.where(keep, pltpu.roll(b, d, axis=1), 0.0)
        b = a * b_prev + b
        a = a * a_prev
        d *= 2
    out = []
    for g in range(groups):
        hg = b[g] + a[g] * h0
        out.append(hg)
        h0 = hg[SUBLANES - 1:SUBLANES, :]
    return jnp.concatenate(out, axis=0)


def _mixers_abc(z, emit_other_matmuls, prm, y_ref, carry_a, carry_r, carry_h):
    (conv_a_w_ref, conv_r_w_ref, conv_r_b_ref, wa_ref, ba_ref, wx_ref, bx_ref,
     lam_ref, gmlp_g_ref, ws_ref, bs_ref) = prm
    r_x, c_v = z(4), z(7)
    a_x, a_b, a_c, a_g = z(0), z(1), z(2), z(3)
    r_g, c_u, c_g = z(5), z(6), z(8)
    rows = a_x.shape[0]

    conv_a = _causal_conv(carry_a, a_c * a_x, conv_a_w_ref)
    y_ref[:, 0:GROUP_W] = (a_b * conv_a * jax.nn.silu(a_g)).astype(y_ref.dtype)

    xb = _causal_conv(carry_r, r_x, conv_r_w_ref) + conv_r_b_ref[...]
    xb16 = xb.astype(BF16)
    r = jax.nn.sigmoid(jnp.dot(xb16, wa_ref[...], preferred_element_type=F32) + ba_ref[...])
    i = jax.nn.sigmoid(jnp.dot(xb16, wx_ref[...], preferred_element_type=F32) + bx_ref[...])
    log_a = (-RG_C * r) * jax.nn.softplus(-lam_ref[...])
    a = jnp.exp(log_a)
    th = jnp.tanh(log_a)
    mult = jnp.sqrt(-2.0 * th / (1.0 - th))
    h = _linear_scan(a, mult * (i * xb), carry_h[0:1, :])
    carry_h[...] = jnp.broadcast_to(h[rows - 1:rows, :], carry_h.shape)
    y_ref[:, GROUP_W:2 * GROUP_W] = (h * jax.nn.silu(r_g)).astype(y_ref.dtype)

    vv = _rms(jax.nn.gelu(c_v), gmlp_g_ref[...])
    t_idx = lax.broadcasted_iota(jnp.int32, ws_ref.shape, 0)
    s_idx = lax.broadcasted_iota(jnp.int32, ws_ref.shape, 1) % GMLP_CHUNK
    ws = jnp.where(s_idx <= t_idx, ws_ref[...], 0.0).astype(BF16)
    emit_other_matmuls()
    for j in range(rows // GMLP_CHUNK):
        sl = slice(j * GMLP_CHUNK, (j + 1) * GMLP_CHUNK)
        vj = vv[sl, :]
        v_heads = jnp.concatenate([_only_head(vj, hh) for hh in range(HEADS)], axis=0).astype(BF16)
        sp = jnp.dot(ws, v_heads, preferred_element_type=F32) + bs_ref[...]
        yc = jax.nn.gelu(c_u[sl, :]) * sp * jax.nn.silu(c_g[sl, :])
        y_ref[sl, 2 * GROUP_W:3 * GROUP_W] = yc.astype(y_ref.dtype)


N_MIX_PARAMS = 11


def _residual(x_ref, yabc_ref, yd_ref, w1_ref, w2_ref, perm_buf):
    y_d = _from_residue_major(perm_buf, yd_ref[...].astype(F32)).astype(BF16)
    acc = jnp.dot(yabc_ref[...], w1_ref[...], preferred_element_type=F32)
    acc += jnp.dot(y_d, w2_ref[...], preferred_element_type=F32)
    return x_ref[...] + acc


def _layer_kernel(has_residual, *refs):
    refs = list(refs)
    take = lambda n: [refs.pop(0) for _ in range(n)]
    if has_residual:
        x_ref, yabc_ref, yd_ref, w1_f32, w2_f32 = take(5)
    else:
        (x_ref,) = take(1)
    g_ref, w_f32 = take(2)
    prm = take(N_MIX_PARAMS)
    if has_residual:
        (xo_ref,) = take(1)
    zq_ref, y_ref, carry_a, carry_r, carry_h, perm_buf, w_ref = refs[:7]
    if has_residual:
        w1_ref, w2_ref = refs[7:]

    @pl.when((pl.program_id(0) == 0) & (pl.program_id(1) == 0))
    def _():
        w_ref[...] = w_f32[...].astype(BF16)
        if has_residual:
            w1_ref[...] = w1_f32[...].astype(BF16)
            w2_ref[...] = w2_f32[...].astype(BF16)

    @pl.when(pl.program_id(1) == 0)
    def _():
        carry_a[...] = jnp.zeros_like(carry_a)
        carry_r[...] = jnp.zeros_like(carry_r)
        carry_h[...] = jnp.zeros_like(carry_h)

    if has_residual:
        x = _residual(x_ref, yabc_ref, yd_ref, w1_ref, w2_ref, perm_buf.at[N_CHUNKS - N_MIX_CHUNKS])
        xo_ref[...] = x
    else:
        x = x_ref[...]
    h = _rms(x, g_ref[...]).astype(BF16)

    def z(c):
        cols = slice(c * GROUP_W, (c + 1) * GROUP_W)
        return jnp.dot(h, w_ref[:, cols], preferred_element_type=F32)

    def attention_chunks():
        for k, c in enumerate(range(N_MIX_CHUNKS, N_CHUNKS)):
            zc = z(c) * Q_SCALE if k == 0 else z(c)
            zq_ref[:, k * GROUP_W:(k + 1) * GROUP_W] = _to_residue_major(perm_buf.at[k], zc)

    _mixers_abc(z, attention_chunks, prm, y_ref, carry_a, carry_r, carry_h)


def _resident(a, l):
    return pl.BlockSpec((None,) + a.shape[1:], lambda *_: (l,) + (0,) * (a.ndim - 1),
                        pipeline_mode=pl.Buffered(1))


def _w_out_specs(w_out, l):
    assert D_MIX_ABC % GROUP_W == 0
    return [pl.BlockSpec((None, D_MIX_ABC, D_MODEL), lambda *_: (l, 0, 0),
                         pipeline_mode=pl.Buffered(1)),
            pl.BlockSpec((None, GROUP_W, D_MODEL), lambda *_: (l, D_MIX_ABC // GROUP_W, 0),
                         pipeline_mode=pl.Buffered(1))]


W_OUT_SCRATCH = [pltpu.VMEM((D_MIX_ABC, D_MODEL), BF16), pltpu.VMEM((GROUP_W, D_MODEL), BF16)]


def _layer(batch, seq, l, x2, prev, g, w_in, w_out, prm):
    tiles = seq // LAYER_ROWS
    n = batch * seq
    has_residual = prev is not None

    def rows(width):
        return pl.BlockSpec((LAYER_ROWS, width), lambda b, t: (b * tiles + t, 0))

    def full(a):
        return pl.BlockSpec(a.shape, lambda b, t: (0,) * a.ndim)

    args = [x2] + (list(prev) + [w_out, w_out] if has_residual else []) + [g, w_in] + list(prm)
    in_specs = [rows(D_MODEL)]
    if has_residual:
        in_specs += [rows(D_MIX_ABC), rows(GROUP_W)] + _w_out_specs(w_out, l - 1)
    in_specs += [full(g), _resident(w_in, l)] + [full(a) for a in prm]
    out_specs = [rows(D_ATTN), rows(D_MIX_ABC)]
    out_shape = [jax.ShapeDtypeStruct((n, D_ATTN), F32),
                 jax.ShapeDtypeStruct((n, D_MIX_ABC), BF16)]
    if has_residual:
        out_specs = [rows(D_MODEL)] + out_specs
        out_shape = [jax.ShapeDtypeStruct((n, D_MODEL), F32)] + out_shape
    return pl.pallas_call(
        functools.partial(_layer_kernel, has_residual),
        grid=(batch, tiles),
        in_specs=in_specs,
        out_specs=out_specs,
        out_shape=out_shape,
        scratch_shapes=[
            pltpu.VMEM((CONV_PAD, GROUP_W), F32),
            pltpu.VMEM((CONV_PAD, GROUP_W), F32),
            pltpu.VMEM((SUBLANES, GROUP_W), F32),
            pltpu.VMEM((N_CHUNKS - N_MIX_CHUNKS + 1, GROUP_W // LANES, LAYER_ROWS, LANES), F32),
            pltpu.VMEM((D_MODEL, D_IN), BF16),
        ] + (W_OUT_SCRATCH if has_residual else []),
        compiler_params=pltpu.CompilerParams(
            dimension_semantics=("arbitrary", "arbitrary"), vmem_limit_bytes=VMEM_LIMIT),
        name="layer_res" if has_residual else "layer_first",
    )(*args)


def _out_final_kernel(x_ref, yabc_ref, yd_ref, w1_f32, w2_f32, g_ref, o_ref,
                      perm_buf, w1_ref, w2_ref):
    @pl.when(pl.program_id(0) == 0)
    def _():
        w1_ref[...] = w1_f32[...].astype(BF16)
        w2_ref[...] = w2_f32[...].astype(BF16)

    o_ref[...] = _rms(_residual(x_ref, yabc_ref, yd_ref, w1_ref, w2_ref, perm_buf), g_ref[...])


def _out_final(x2, y_abc, y_d, w_out, l, g):
    n = x2.shape[0]
    rows = lambda width: pl.BlockSpec((FINAL_ROWS, width), lambda i: (i, 0))
    full = lambda a: pl.BlockSpec(a.shape, lambda i: (0,) * a.ndim)
    return pl.pallas_call(
        _out_final_kernel,
        grid=(n // FINAL_ROWS,),
        in_specs=[rows(D_MODEL), rows(D_MIX_ABC), rows(GROUP_W)] + _w_out_specs(w_out, l) + [full(g)],
        out_specs=rows(D_MODEL),
        out_shape=jax.ShapeDtypeStruct((n, D_MODEL), F32),
        scratch_shapes=[pltpu.VMEM((GROUP_W // LANES, FINAL_ROWS, LANES), F32)] + W_OUT_SCRATCH,
        compiler_params=pltpu.CompilerParams(
            dimension_semantics=("arbitrary",), vmem_limit_bytes=VMEM_LIMIT),
        name="out_final",
    )(x2, y_abc, y_d, w_out, w_out, g)


def _attn_kernel(q0, q1, kp0, kp1, kc0, kc1, vp0, vp1, vc0, vc1, g_ref, y_ref,
                 bias_ref, acc_ref, m_ref, l_ref):
    q_ref, kp_ref, kc_ref, vp_ref, vc_ref = (q0, q1), (kp0, kp1), (kc0, kc1), (vp0, vp1), (vc0, vc1)
    first_tile = (pl.program_id(1) == 0).astype(jnp.int32)
    per = SUBTILE // RESIDUES

    def ld(halves, chunks):
        return jnp.concatenate(
            [jnp.concatenate([h[c, :] for h in halves], axis=1) for c, _ in chunks], axis=0)

    def ld_state(ref, chunks):
        return ld((ref.at[0], ref.at[1]), chunks)

    def st_rows(store, chunks, val):
        start = 0
        for c, size in chunks:
            store(c, val[start:start + size, :])
            start += size

    def st_state(ref, chunks, val):
        def store(c, v):
            ref[0, c, :] = v[:, :LANES]
            ref[1, c, :] = v[:, LANES:]
        st_rows(store, chunks, val)

    @pl.when((pl.program_id(0) == 0) & (pl.program_id(1) == 0))
    def _():
        qi = lax.broadcasted_iota(jnp.int32, (ATTN_BLOCK, 2 * ATTN_BLOCK), 0)
        kj = lax.broadcasted_iota(jnp.int32, (ATTN_BLOCK, 2 * ATTN_BLOCK), 1)
        q_major = RESIDUES * (qi % (per // RESIDUES)) + qi // (per // RESIDUES)
        k_major = (RESIDUES * (kj % (2 * per // RESIDUES) - per // RESIDUES)
                   + kj // (2 * per // RESIDUES))
        for p, dil in enumerate(ATTN_DILATIONS):
            steps = q_major - k_major if dil == 1 else qi + ATTN_BLOCK - kj
            in_block = k_major >= 0 if dil == 1 else kj >= ATTN_BLOCK
            band = (steps >= 0) & (steps <= ATTN_BLOCK)
            for hh in range(HEADS):
                slope = 2.0 ** (-8.0 * (hh + 1) / HEADS)
                bias = (-slope * LOG2_E) * (steps * dil).astype(F32)
                bias_ref[(p * HEADS + hh) * 2] = jnp.where(band, bias, MASKED)
                bias_ref[(p * HEADS + hh) * 2 + 1] = jnp.where(band & in_block, bias, MASKED)

    lower_head = _lower_head_mask(ATTN_BLOCK)

    def expand(cols):
        return jnp.concatenate(
            [jnp.where(lower_head, cols[2 * i], cols[2 * i + 1]) for i in range(HEADS // 2)], axis=1)

    def pick_heads(per_head):
        return jnp.concatenate(
            [jnp.where(lower_head, per_head[2 * i][:, i * LANES:(i + 1) * LANES],
                       per_head[2 * i + 1][:, i * LANES:(i + 1) * LANES])
             for i in range(HEADS // 2)], axis=1)

    def run_blocks(mode, specs):
        def scores_of(spec):
            p, first, rows, kk, vv = spec
            q = ld(q_ref, rows)
            q_heads = jnp.concatenate([_only_head(q, hh) for hh in range(HEADS)],
                                      axis=0).astype(BF16)
            return lax.dot_general(q_heads, kk.astype(BF16), (((1,), (1,)), ((), ())),
                                   preferred_element_type=F32)

        def softmax_of(spec, s_all):
            p, first = spec[:2]
            probs, m_cols, l_cols = [], [], []
            for hh in range(HEADS):
                s = s_all[hh * ATTN_BLOCK:(hh + 1) * ATTN_BLOCK, :]
                s = s + bias_ref[(p * HEADS + hh) * 2 + first]
                m_h = jnp.max(s, axis=-1, keepdims=True)
                e = jnp.exp2(s - m_h)
                probs.append(e.astype(BF16))
                m_cols.append(m_h)
                l_cols.append(jnp.sum(e, axis=-1, keepdims=True))
            return jnp.concatenate(probs, axis=0), m_cols, l_cols

        def values_of(spec, p_all, m_cols, l_cols):
            p, first, rows, kk, vv = spec
            o_all = jnp.dot(p_all, vv.astype(BF16), preferred_element_type=F32)
            acc = pick_heads([o_all[hh * ATTN_BLOCK:(hh + 1) * ATTN_BLOCK, :] for hh in range(HEADS)])
            m_b = expand(m_cols)
            l_b = expand(l_cols)
            if mode != "init":
                m_s, l_s, acc_s = ld_state(m_ref, rows), ld_state(l_ref, rows), ld_state(acc_ref, rows)
                m_n = jnp.maximum(m_s, m_b)
                w_s = jnp.exp2(m_s - m_n)
                w_b = jnp.exp2(m_b - m_n)
                acc = acc_s * w_s + acc * w_b
                l_b = l_s * w_s + l_b * w_b
                m_b = m_n
            if mode == "final":
                gate = jnp.concatenate([g_ref[c, :] for c, _ in rows], axis=0)
                out = (acc / l_b * jax.nn.silu(gate)).astype(y_ref.dtype)

                def store(c, v):
                    y_ref[c, :] = v
                st_rows(store, rows, out)
            else:
                st_state(m_ref, rows, m_b)
                st_state(l_ref, rows, l_b)
                st_state(acc_ref, rows, acc)

        scores = {i: scores_of(specs[i]) for i in range(min(SCORE_LOOKAHEAD, len(specs)))}
        for i, spec in enumerate(specs):
            soft = softmax_of(spec, scores.pop(i))
            if i + SCORE_LOOKAHEAD < len(specs):
                scores[i + SCORE_LOOKAHEAD] = scores_of(specs[i + SCORE_LOOKAHEAD])
            values_of(spec, *soft)

    def keys_values(prev_refs, prev_rows, rows):
        k_src, v_src = (kp_ref, vp_ref) if prev_refs else (kc_ref, vc_ref)
        kk = jnp.concatenate([ld(k_src, prev_rows), ld(kc_ref, rows)], axis=0)
        vv = jnp.concatenate([ld(v_src, prev_rows), ld(vc_ref, rows)], axis=0)
        return kk, vv


    def spec16(r):
        a, b = r // RESIDUES, r % RESIDUES
        rows = [(pl.ds(t * SUBTILE + b * per + a, per // RESIDUES, stride=RESIDUES), per // RESIDUES)
                for t in range(ATTN_TILE // SUBTILE)]
        return (2, first_tile, rows) + keys_values(1, rows, rows)

    def body16(i, carry):
        run_blocks("init", [spec16(GROUP_STRIDE16 * i + j) for j in range(GROUP_STRIDE16)])
        return carry
    lax.fori_loop(0, 16 // GROUP_STRIDE16, body16, 0)

    def spec4(t, b):
        rows = [(pl.ds(t * SUBTILE + b * per, per), per)]
        t_prev = (t - 1) % (ATTN_TILE // SUBTILE)
        prev_rows = [(pl.ds(t_prev * SUBTILE + b * per, per), per)]
        first = first_tile if t == 0 else 0
        return (1, first, rows) + keys_values(t == 0, prev_rows, rows)

    for t0 in range(0, ATTN_TILE // SUBTILE, SUBTILES_PER_GROUP):
        run_blocks("merge", [spec4(t, b) for t in range(t0, t0 + SUBTILES_PER_GROUP)
                             for b in range(RESIDUES)])

    n_sub = SUBTILE // ATTN_BLOCK
    blk = per // n_sub

    def spec1(t, u):
        base = t * SUBTILE
        rows = [(pl.ds(base + b * per + u * blk, blk), blk) for b in range(RESIDUES)]
        if u > 0:
            key_rows = [(pl.ds(base + b * per + (u - 1) * blk, 2 * blk), 2 * blk)
                        for b in range(RESIDUES)]
            return (0, 0, rows, ld(kc_ref, key_rows), ld(vc_ref, key_rows))
        first_of_tile = t == 0
        k_src, v_src = (kp_ref, vp_ref) if first_of_tile else (kc_ref, vc_ref)
        prev_base = (ATTN_TILE - SUBTILE) if first_of_tile else base - SUBTILE
        kk, vv = [], []
        for b in range(RESIDUES):
            before = [(pl.ds(prev_base + b * per + per - blk, blk), blk)]
            kk += [ld(k_src, before), ld(kc_ref, rows[b:b + 1])]
            vv += [ld(v_src, before), ld(vc_ref, rows[b:b + 1])]
        first = first_tile if first_of_tile else 0
        return (0, first, rows, jnp.concatenate(kk, axis=0), jnp.concatenate(vv, axis=0))

    for t0 in range(0, ATTN_TILE // SUBTILE, SUBTILES_PER_GROUP):
        run_blocks("final", [spec1(t, u) for t in range(t0, t0 + SUBTILES_PER_GROUP)
                             for u in range(n_sub)])


def _attention(zq, batch, seq):
    tiles = seq // ATTN_TILE
    halves = GROUP_W // LANES

    def cur(c):
        return [pl.BlockSpec((ATTN_TILE, LANES), lambda b, n, c=c, h=h: (b * tiles + n, c * halves + h))
                for h in range(halves)]

    def prev(c):
        return [pl.BlockSpec((ATTN_TILE, LANES),
                             lambda b, n, c=c, h=h: (b * tiles + jnp.maximum(n - 1, 0), c * halves + h))
                for h in range(halves)]

    in_specs = cur(0) + prev(1) + cur(1) + prev(2) + cur(2) + [
        pl.BlockSpec((ATTN_TILE, GROUP_W), lambda b, n: (b * tiles + n, 3))]
    state = pltpu.VMEM((halves, ATTN_TILE, LANES), F32)
    return pl.pallas_call(
        _attn_kernel,
        grid=(batch, tiles),
        in_specs=in_specs,
        out_specs=pl.BlockSpec((ATTN_TILE, GROUP_W), lambda b, n: (b * tiles + n, 0)),
        out_shape=jax.ShapeDtypeStruct((batch * seq, GROUP_W), BF16),
        scratch_shapes=[
            pltpu.VMEM((len(ATTN_DILATIONS) * HEADS * 2, ATTN_BLOCK, 2 * ATTN_BLOCK), F32),
            state, state, state,
        ],
        compiler_params=pltpu.CompilerParams(
            dimension_semantics=("arbitrary", "arbitrary"), vmem_limit_bytes=VMEM_LIMIT),
        name="dilated_attention",
    )(*([zq] * len(in_specs)))


def _block_diag(w):
    h, d, _ = w.shape
    eye = jnp.eye(h, dtype=w.dtype)
    return jnp.einsum("hij,hg->higj", w, eye).reshape(h * d, h * d)


def kernel(x, norm_g, w_in, conv_a_w, conv_r_w, conv_r_b, lru_wa, lru_ba, lru_wx, lru_bx,
           lru_lambda, gmlp_norm_g, gmlp_ws, gmlp_bs, w_out, final_g):
    batch, seq, _ = x.shape
    depth = w_in.shape[0]
    x2 = x.reshape(batch * seq, D_MODEL)
    row = lambda v: v.reshape(1, -1)
    prev = None
    for l in range(depth):
        prm = [
            conv_a_w[l], conv_r_w[l], row(conv_r_b[l]),
            _block_diag(lru_wa[l]).astype(BF16), row(lru_ba[l]),
            _block_diag(lru_wx[l]).astype(BF16), row(lru_bx[l]),
            row(lru_lambda[l]), row(gmlp_norm_g[l]),
            jnp.transpose(gmlp_ws[l], (1, 0, 2)).reshape(GMLP_CHUNK, HEADS * GMLP_CHUNK),
            jnp.repeat(jnp.transpose(gmlp_bs[l]), HEAD_DIM, axis=1),
        ]
        outs = _layer(batch, seq, l, x2, prev, row(norm_g[l]), w_in, w_out, prm)
        if prev is not None:
            x2 = outs[0]
        zq, y_abc = outs[-2:]
        prev = (y_abc, _attention(zq, batch, seq))
    out = _out_final(x2, *prev, w_out, depth - 1, row(final_g))
    return out.reshape(batch, seq, D_MODEL)
```

```python
import functools
import math

import jax
import jax.numpy as jnp
from jax import lax
from jax.experimental import pallas as pl
from jax.experimental.pallas import tpu as pltpu

F32 = jnp.float32
BF16 = jnp.bfloat16

D_MODEL = 1024
GROUP_W = 256
HEADS = 4
HEAD_DIM = 64
N_CHUNKS = 13
N_MIX_CHUNKS = 9
D_IN = N_CHUNKS * GROUP_W
D_MIX_ABC = 3 * GROUP_W
D_ATTN = (N_CHUNKS - N_MIX_CHUNKS) * GROUP_W
RG_C = 8.0
GMLP_CHUNK = 128
ATTN_BLOCK = 128
LANES = 128
ATTN_DILATIONS = (1, 4, 16)
ATTN_TILE = ATTN_BLOCK * 16
GROUP_STRIDE16 = 16
SUBTILES_PER_GROUP = 4
SCORE_LOOKAHEAD = 1
NORM_EPS = 1e-6
MASKED = -1e30
LOG2_E = math.log2(math.e)
Q_SCALE = LOG2_E / math.sqrt(HEAD_DIM)

LAYER_ROWS = 512
RESIDUES = 4
SUBTILE = LAYER_ROWS
assert ATTN_TILE % SUBTILE == 0 and SUBTILE == 4 * ATTN_BLOCK
FINAL_ROWS = 2 * SUBTILE
SUBLANES = 8
CONV_PAD = SUBLANES
VMEM_LIMIT = 56 * 1024 * 1024


def _rms(x, g):
    return x * lax.rsqrt(jnp.mean(x * x, axis=-1, keepdims=True) + NORM_EPS) * g


def _lower_head_mask(rows):
    return lax.broadcasted_iota(jnp.int32, (rows, LANES), 1) < HEAD_DIM


def _only_head(x, hh):
    lower_head = _lower_head_mask(x.shape[0])
    half, lower = hh // 2, hh % 2 == 0
    part = x[:, half * LANES:(half + 1) * LANES]
    part = jnp.where(lower_head, part, 0.0) if lower else jnp.where(lower_head, 0.0, part)
    zeros = jnp.zeros_like(part)
    return jnp.concatenate([part if i == half else zeros for i in range(HEADS // 2)], axis=1)


def _to_residue_major(buf, x):
    per = x.shape[0] // RESIDUES
    for half in range(GROUP_W // LANES):
        buf[half] = x[:, half * LANES:(half + 1) * LANES]
    return jnp.concatenate(
        [jnp.concatenate([buf[half, pl.ds(b, per, stride=RESIDUES), :]
                          for half in range(GROUP_W // LANES)], axis=1)
         for b in range(RESIDUES)], axis=0)


def _from_residue_major(buf, x):
    per = SUBTILE // RESIDUES
    for start in range(0, x.shape[0], SUBTILE):
        for b in range(RESIDUES):
            for half in range(GROUP_W // LANES):
                buf[half, pl.ds(start + b, per, stride=RESIDUES), :] = (
                    x[start + b * per:start + (b + 1) * per, half * LANES:(half + 1) * LANES])
    return jnp.concatenate([buf[half] for half in range(GROUP_W // LANES)], axis=1)


def _causal_conv(carry_ref, x, w_ref):
    rows = x.shape[0]
    taps = w_ref.shape[0]
    history = carry_ref[...]
    carry_ref[...] = x[rows - CONV_PAD:, :]
    row = lax.broadcasted_iota(jnp.int32, history.shape, 0)
    out = x * w_ref[taps - 1:taps, :]
    for k in range(taps - 1):
        shift = taps - 1 - k
        rolled = pltpu.roll(x, shift, axis=0)
        head = jnp.where(row < shift, pltpu.roll(history, shift, axis=0), rolled[0:CONV_PAD])
        shifted = jnp.concatenate([head, rolled[CONV_PAD:]], axis=0)
        out += shifted * w_ref[k:k + 1, :]
    return out


def _linear_scan(a, b, h0):
    rows, cols = a.shape
    groups = rows // SUBLANES
    a = a.reshape(groups, SUBLANES, cols)
    b = b.reshape(groups, SUBLANES, cols)
    row = lax.broadcasted_iota(jnp.int32, a.shape, 1)
    d = 1
    while d < SUBLANES:
        keep = row >= d
        a_prev = jnp.where(keep, pltpu.roll(a, d, axis=1), 1.0)
        b_prev = jnp.where(keep, pltpu.roll(b, d, axis=1), 0.0)
        b = a * b_prev + b
        a = a * a_prev
        d *= 2
    out = []
    for g in range(groups):
        hg = b[g] + a[g] * h0
        out.append(hg)
        h0 = hg[SUBLANES - 1:SUBLANES, :]
    return jnp.concatenate(out, axis=0)


def _mixers_abc(z, emit_other_matmuls, prm, y_ref, carry_a, carry_r, carry_h):
    (conv_a_w_ref, conv_r_w_ref, conv_r_b_ref, wa_ref, ba_ref, wx_ref, bx_ref,
     lam_ref, gmlp_g_ref, ws_ref, bs_ref) = prm
    r_x = z(4)
    rows = r_x.shape[0]
    xb = _causal_conv(carry_r, r_x, conv_r_w_ref) + conv_r_b_ref[...]
    xb16 = xb.astype(BF16)
    emit_other_matmuls()
    r = jax.nn.sigmoid(jnp.dot(xb16, wa_ref[...], preferred_element_type=F32) + ba_ref[...])
    i = jax.nn.sigmoid(jnp.dot(xb16, wx_ref[...], preferred_element_type=F32) + bx_ref[...])
    c_v, c_u, c_g = z(7), z(6), z(8)
    a_x, a_b, a_c, a_g = z(0), z(1), z(2), z(3)
    r_g = z(5)

    conv_a = _causal_conv(carry_a, a_c * a_x, conv_a_w_ref)
    y_ref[:, 0:GROUP_W] = (a_b * conv_a * jax.nn.silu(a_g)).astype(y_ref.dtype)

    log_a = (-RG_C * r) * jax.nn.softplus(-lam_ref[...])
    a = jnp.exp(log_a)
    th = jnp.tanh(log_a)
    mult = jnp.sqrt(-2.0 * th / (1.0 - th))
    h = _linear_scan(a, mult * (i * xb), carry_h[0:1, :])
    carry_h[...] = jnp.broadcast_to(h[rows - 1:rows, :], carry_h.shape)
    y_ref[:, GROUP_W:2 * GROUP_W] = (h * jax.nn.silu(r_g)).astype(y_ref.dtype)

    vv = _rms(jax.nn.gelu(c_v), gmlp_g_ref[...])
    t_idx = lax.broadcasted_iota(jnp.int32, ws_ref.shape, 0)
    s_idx = lax.broadcasted_iota(jnp.int32, ws_ref.shape, 1) % GMLP_CHUNK
    ws = jnp.where(s_idx <= t_idx, ws_ref[...], 0.0).astype(BF16)
    for j in range(rows // GMLP_CHUNK):
        sl = slice(j * GMLP_CHUNK, (j + 1) * GMLP_CHUNK)
        vj = vv[sl, :]
        v_heads = jnp.concatenate([_only_head(vj, hh) for hh in range(HEADS)], axis=0).astype(BF16)
        sp = jnp.dot(ws, v_heads, preferred_element_type=F32) + bs_ref[...]
        yc = jax.nn.gelu(c_u[sl, :]) * sp * jax.nn.silu(c_g[sl, :])
        y_ref[sl, 2 * GROUP_W:3 * GROUP_W] = yc.astype(y_ref.dtype)


N_MIX_PARAMS = 11


def _residual(x_ref, yabc_ref, yd_ref, w1_ref, w2_ref, perm_buf):
    y_d = _from_residue_major(perm_buf, yd_ref[...].astype(F32)).astype(BF16)
    acc = jnp.dot(yabc_ref[...], w1_ref[...], preferred_element_type=F32)
    acc += jnp.dot(y_d, w2_ref[...], preferred_element_type=F32)
    return x_ref[...] + acc


def _layer_kernel(has_residual, *refs):
    refs = list(refs)
    take = lambda n: [refs.pop(0) for _ in range(n)]
    if has_residual:
        x_ref, yabc_ref, yd_ref, w1_f32, w2_f32 = take(5)
    else:
        (x_ref,) = take(1)
    g_ref, w_f32 = take(2)
    prm = take(N_MIX_PARAMS)
    if has_residual:
        (xo_ref,) = take(1)
    zq_ref, y_ref, carry_a, carry_r, carry_h, perm_buf, w_ref = refs[:7]
    if has_residual:
        w1_ref, w2_ref = refs[7:]

    @pl.when((pl.program_id(0) == 0) & (pl.program_id(1) == 0))
    def _():
        w_ref[...] = w_f32[...].astype(BF16)
        if has_residual:
            w1_ref[...] = w1_f32[...].astype(BF16)
            w2_ref[...] = w2_f32[...].astype(BF16)

    @pl.when(pl.program_id(1) == 0)
    def _():
        carry_a[...] = jnp.zeros_like(carry_a)
        carry_r[...] = jnp.zeros_like(carry_r)
        carry_h[...] = jnp.zeros_like(carry_h)

    if has_residual:
        x = _residual(x_ref, yabc_ref, yd_ref, w1_ref, w2_ref, perm_buf.at[N_CHUNKS - N_MIX_CHUNKS])
        xo_ref[...] = x
    else:
        x = x_ref[...]
    h = _rms(x, g_ref[...]).astype(BF16)

    def z(c):
        cols = slice(c * GROUP_W, (c + 1) * GROUP_W)
        return jnp.dot(h, w_ref[:, cols], preferred_element_type=F32)

    def attention_chunks():
        for k, c in enumerate(range(N_MIX_CHUNKS, N_CHUNKS)):
            zc = z(c) * Q_SCALE if k == 0 else z(c)
            zq_ref[:, k * GROUP_W:(k + 1) * GROUP_W] = _to_residue_major(perm_buf.at[k], zc)

    _mixers_abc(z, attention_chunks, prm, y_ref, carry_a, carry_r, carry_h)


def _resident(a, l):
    return pl.BlockSpec((None,) + a.shape[1:], lambda *_: (l,) + (0,) * (a.ndim - 1),
                        pipeline_mode=pl.Buffered(1))


def _w_out_specs(w_out, l):
    assert D_MIX_ABC % GROUP_W == 0
    return [pl.BlockSpec((None, D_MIX_ABC, D_MODEL), lambda *_: (l, 0, 0),
                         pipeline_mode=pl.Buffered(1)),
            pl.BlockSpec((None, GROUP_W, D_MODEL), lambda *_: (l, D_MIX_ABC // GROUP_W, 0),
                         pipeline_mode=pl.Buffered(1))]


W_OUT_SCRATCH = [pltpu.VMEM((D_MIX_ABC, D_MODEL), BF16), pltpu.VMEM((GROUP_W, D_MODEL), BF16)]


def _layer(batch, seq, l, x2, prev, g, w_in, w_out, prm):
    tiles = seq // LAYER_ROWS
    n = batch * seq
    has_residual = prev is not None

    def rows(width):
        return pl.BlockSpec((LAYER_ROWS, width), lambda b, t: (b * tiles + t, 0))

    def full(a):
        return pl.BlockSpec(a.shape, lambda b, t: (0,) * a.ndim)

    args = [x2] + (list(prev) + [w_out, w_out] if has_residual else []) + [g, w_in] + list(prm)
    in_specs = [rows(D_MODEL)]
    if has_residual:
        in_specs += [rows(D_MIX_ABC), rows(GROUP_W)] + _w_out_specs(w_out, l - 1)
    in_specs += [full(g), _resident(w_in, l)] + [full(a) for a in prm]
    out_specs = [rows(D_ATTN), rows(D_MIX_ABC)]
    out_shape = [jax.ShapeDtypeStruct((n, D_ATTN), F32),
                 jax.ShapeDtypeStruct((n, D_MIX_ABC), BF16)]
    if has_residual:
        out_specs = [rows(D_MODEL)] + out_specs
        out_shape = [jax.ShapeDtypeStruct((n, D_MODEL), F32)] + out_shape
    return pl.pallas_call(
        functools.partial(_layer_kernel, has_residual),
        grid=(batch, tiles),
        in_specs=in_specs,
        out_specs=out_specs,
        out_shape=out_shape,
        scratch_shapes=[
            pltpu.VMEM((CONV_PAD, GROUP_W), F32),
            pltpu.VMEM((CONV_PAD, GROUP_W), F32),
            pltpu.VMEM((SUBLANES, GROUP_W), F32),
            pltpu.VMEM((N_CHUNKS - N_MIX_CHUNKS + 1, GROUP_W // LANES, LAYER_ROWS, LANES), F32),
            pltpu.VMEM((D_MODEL, D_IN), BF16),
        ] + (W_OUT_SCRATCH if has_residual else []),
        compiler_params=pltpu.CompilerParams(
            dimension_semantics=("arbitrary", "arbitrary"), vmem_limit_bytes=VMEM_LIMIT),
        name="layer_res" if has_residual else "layer_first",
    )(*args)


def _out_final_kernel(x_ref, yabc_ref, yd_ref, w1_f32, w2_f32, g_ref, o_ref,
                      perm_buf, w1_ref, w2_ref):
    @pl.when(pl.program_id(0) == 0)
    def _():
        w1_ref[...] = w1_f32[...].astype(BF16)
        w2_ref[...] = w2_f32[...].astype(BF16)

    o_ref[...] = _rms(_residual(x_ref, yabc_ref, yd_ref, w1_ref, w2_ref, perm_buf), g_ref[...])


def _out_final(x2, y_abc, y_d, w_out, l, g):
    n = x2.shape[0]
    rows = lambda width: pl.BlockSpec((FINAL_ROWS, width), lambda i: (i, 0))
    full = lambda a: pl.BlockSpec(a.shape, lambda i: (0,) * a.ndim)
    return pl.pallas_call(
        _out_final_kernel,
        grid=(n // FINAL_ROWS,),
        in_specs=[rows(D_MODEL), rows(D_MIX_ABC), rows(GROUP_W)] + _w_out_specs(w_out, l) + [full(g)],
        out_specs=rows(D_MODEL),
        out_shape=jax.ShapeDtypeStruct((n, D_MODEL), F32),
        scratch_shapes=[pltpu.VMEM((GROUP_W // LANES, FINAL_ROWS, LANES), F32)] + W_OUT_SCRATCH,
        compiler_params=pltpu.CompilerParams(
            dimension_semantics=("arbitrary",), vmem_limit_bytes=VMEM_LIMIT),
        name="out_final",
    )(x2, y_abc, y_d, w_out, w_out, g)


def _attn_kernel(q0, q1, kp0, kp1, kc0, kc1, vp0, vp1, vc0, vc1, g_ref, y_ref,
                 bias_ref, acc_ref, m_ref, l_ref):
    q_ref, kp_ref, kc_ref, vp_ref, vc_ref = (q0, q1), (kp0, kp1), (kc0, kc1), (vp0, vp1), (vc0, vc1)
    first_tile = (pl.program_id(1) == 0).astype(jnp.int32)
    per = SUBTILE // RESIDUES

    def ld(halves, chunks):
        return jnp.concatenate(
            [jnp.concatenate([h[c, :] for h in halves], axis=1) for c, _ in chunks], axis=0)

    def ld_state(ref, chunks):
        return ld((ref.at[0], ref.at[1]), chunks)

    def st_rows(store, chunks, val):
        start = 0
        for c, size in chunks:
            store(c, val[start:start + size, :])
            start += size

    def st_state(ref, chunks, val):
        def store(c, v):
            ref[0, c, :] = v[:, :LANES]
            ref[1, c, :] = v[:, LANES:]
        st_rows(store, chunks, val)

    @pl.when((pl.program_id(0) == 0) & (pl.program_id(1) == 0))
    def _():
        qi = lax.broadcasted_iota(jnp.int32, (ATTN_BLOCK, 2 * ATTN_BLOCK), 0)
        kj = lax.broadcasted_iota(jnp.int32, (ATTN_BLOCK, 2 * ATTN_BLOCK), 1)
        q_major = RESIDUES * (qi % (per // RESIDUES)) + qi // (per // RESIDUES)
        k_major = (RESIDUES * (kj % (2 * per // RESIDUES) - per // RESIDUES)
                   + kj // (2 * per // RESIDUES))
        for p, dil in enumerate(ATTN_DILATIONS):
            steps = q_major - k_major if dil == 1 else qi + ATTN_BLOCK - kj
            in_block = k_major >= 0 if dil == 1 else kj >= ATTN_BLOCK
            band = (steps >= 0) & (steps <= ATTN_BLOCK)
            for hh in range(HEADS):
                slope = 2.0 ** (-8.0 * (hh + 1) / HEADS)
                bias = (-slope * LOG2_E) * (steps * dil).astype(F32)
                bias_ref[(p * HEADS + hh) * 2] = jnp.where(band, bias, MASKED)
                bias_ref[(p * HEADS + hh) * 2 + 1] = jnp.where(band & in_block, bias, MASKED)

    lower_head = _lower_head_mask(ATTN_BLOCK)

    def expand(cols):
        return jnp.concatenate(
            [jnp.where(lower_head, cols[2 * i], cols[2 * i + 1]) for i in range(HEADS // 2)], axis=1)

    def pick_heads(per_head):
        return jnp.concatenate(
            [jnp.where(lower_head, per_head[2 * i][:, i * LANES:(i + 1) * LANES],
                       per_head[2 * i + 1][:, i * LANES:(i + 1) * LANES])
             for i in range(HEADS // 2)], axis=1)

    def run_blocks(mode, specs):
        def scores_of(spec):
            p, first, rows, kk, vv = spec
            q = ld(q_ref, rows)
            q_heads = jnp.concatenate([_only_head(q, hh) for hh in range(HEADS)],
                                      axis=0).astype(BF16)
            return lax.dot_general(q_heads, kk.astype(BF16), (((1,), (1,)), ((), ())),
                                   preferred_element_type=F32)

        def softmax_of(spec, s_all):
            p, first = spec[:2]
            probs, m_cols, l_cols = [], [], []
            for hh in range(HEADS):
                s = s_all[hh * ATTN_BLOCK:(hh + 1) * ATTN_BLOCK, :]
                s = s + bias_ref[(p * HEADS + hh) * 2 + first]
                m_h = jnp.max(s, axis=-1, keepdims=True)
                e = jnp.exp2(s - m_h)
                probs.append(e.astype(BF16))
                m_cols.append(m_h)
                l_cols.append(jnp.sum(e, axis=-1, keepdims=True))
            return jnp.concatenate(probs, axis=0), m_cols, l_cols

        def values_of(spec, p_all, m_cols, l_cols):
            p, first, rows, kk, vv = spec
            o_all = jnp.dot(p_all, vv.astype(BF16), preferred_element_type=F32)
            acc = pick_heads([o_all[hh * ATTN_BLOCK:(hh + 1) * ATTN_BLOCK, :] for hh in range(HEADS)])
            m_b = expand(m_cols)
            l_b = expand(l_cols)
            if mode != "init":
                m_s, l_s, acc_s = ld_state(m_ref, rows), ld_state(l_ref, rows), ld_state(acc_ref, rows)
                m_n = jnp.maximum(m_s, m_b)
                w_s = jnp.exp2(m_s - m_n)
                w_b = jnp.exp2(m_b - m_n)
                acc = acc_s * w_s + acc * w_b
                l_b = l_s * w_s + l_b * w_b
                m_b = m_n
            if mode == "final":
                gate = jnp.concatenate([g_ref[c, :] for c, _ in rows], axis=0)
                out = (acc / l_b * jax.nn.silu(gate)).astype(y_ref.dtype)

                def store(c, v):
                    y_ref[c, :] = v
                st_rows(store, rows, out)
            else:
                st_state(m_ref, rows, m_b)
                st_state(l_ref, rows, l_b)
                st_state(acc_ref, rows, acc)

        scores = {i: scores_of(specs[i]) for i in range(min(SCORE_LOOKAHEAD, len(specs)))}
        for i, spec in enumerate(specs):
            soft = softmax_of(spec, scores.pop(i))
            if i + SCORE_LOOKAHEAD < len(specs):
                scores[i + SCORE_LOOKAHEAD] = scores_of(specs[i + SCORE_LOOKAHEAD])
            values_of(spec, *soft)

    def keys_values(prev_refs, prev_rows, rows):
        k_src, v_src = (kp_ref, vp_ref) if prev_refs else (kc_ref, vc_ref)
        kk = jnp.concatenate([ld(k_src, prev_rows), ld(kc_ref, rows)], axis=0)
        vv = jnp.concatenate([ld(v_src, prev_rows), ld(vc_ref, rows)], axis=0)
        return kk, vv


    def spec16(r):
        a, b = r // RESIDUES, r % RESIDUES
        rows = [(pl.ds(t * SUBTILE + b * per + a, per // RESIDUES, stride=RESIDUES), per // RESIDUES)
                for t in range(ATTN_TILE // SUBTILE)]
        return (2, first_tile, rows) + keys_values(1, rows, rows)

    def body16(i, carry):
        run_blocks("init", [spec16(GROUP_STRIDE16 * i + j) for j in range(GROUP_STRIDE16)])
        return carry
    lax.fori_loop(0, 16 // GROUP_STRIDE16, body16, 0)

    def spec4(t, b):
        rows = [(pl.ds(t * SUBTILE + b * per, per), per)]
        t_prev = (t - 1) % (ATTN_TILE // SUBTILE)
        prev_rows = [(pl.ds(t_prev * SUBTILE + b * per, per), per)]
        first = first_tile if t == 0 else 0
        return (1, first, rows) + keys_values(t == 0, prev_rows, rows)

    for t0 in range(0, ATTN_TILE // SUBTILE, SUBTILES_PER_GROUP):
        run_blocks("merge", [spec4(t, b) for t in range(t0, t0 + SUBTILES_PER_GROUP)
                             for b in range(RESIDUES)])

    n_sub = SUBTILE // ATTN_BLOCK
    blk = per // n_sub

    def spec1(t, u):
        base = t * SUBTILE
        rows = [(pl.ds(base + b * per + u * blk, blk), blk) for b in range(RESIDUES)]
        if u > 0:
            key_rows = [(pl.ds(base + b * per + (u - 1) * blk, 2 * blk), 2 * blk)
                        for b in range(RESIDUES)]
            return (0, 0, rows, ld(kc_ref, key_rows), ld(vc_ref, key_rows))
        first_of_tile = t == 0
        k_src, v_src = (kp_ref, vp_ref) if first_of_tile else (kc_ref, vc_ref)
        prev_base = (ATTN_TILE - SUBTILE) if first_of_tile else base - SUBTILE
        kk, vv = [], []
        for b in range(RESIDUES):
            before = [(pl.ds(prev_base + b * per + per - blk, blk), blk)]
            kk += [ld(k_src, before), ld(kc_ref, rows[b:b + 1])]
            vv += [ld(v_src, before), ld(vc_ref, rows[b:b + 1])]
        first = first_tile if first_of_tile else 0
        return (0, first, rows, jnp.concatenate(kk, axis=0), jnp.concatenate(vv, axis=0))

    for t0 in range(0, ATTN_TILE // SUBTILE, SUBTILES_PER_GROUP):
        run_blocks("final", [spec1(t, u) for t in range(t0, t0 + SUBTILES_PER_GROUP)
                             for u in range(n_sub)])


def _attention(zq, batch, seq):
    tiles = seq // ATTN_TILE
    halves = GROUP_W // LANES

    def cur(c):
        return [pl.BlockSpec((ATTN_TILE, LANES), lambda b, n, c=c, h=h: (b * tiles + n, c * halves + h))
                for h in range(halves)]

    def prev(c):
        return [pl.BlockSpec((ATTN_TILE, LANES),
                             lambda b, n, c=c, h=h: (b * tiles + jnp.maximum(n - 1, 0), c * halves + h))
                for h in range(halves)]

    in_specs = cur(0) + prev(1) + cur(1) + prev(2) + cur(2) + [
        pl.BlockSpec((ATTN_TILE, GROUP_W), lambda b, n: (b * tiles + n, 3))]
    state = pltpu.VMEM((halves, ATTN_TILE, LANES), F32)
    return pl.pallas_call(
        _attn_kernel,
        grid=(batch, tiles),
        in_specs=in_specs,
        out_specs=pl.BlockSpec((ATTN_TILE, GROUP_W), lambda b, n: (b * tiles + n, 0)),
        out_shape=jax.ShapeDtypeStruct((batch * seq, GROUP_W), BF16),
        scratch_shapes=[
            pltpu.VMEM((len(ATTN_DILATIONS) * HEADS * 2, ATTN_BLOCK, 2 * ATTN_BLOCK), F32),
            state, state, state,
        ],
        compiler_params=pltpu.CompilerParams(
            dimension_semantics=("arbitrary", "arbitrary"), vmem_limit_bytes=VMEM_LIMIT),
        name="dilated_attention",
    )(*([zq] * len(in_specs)))


def _block_diag(w):
    h, d, _ = w.shape
    eye = jnp.eye(h, dtype=w.dtype)
    return jnp.einsum("hij,hg->higj", w, eye).reshape(h * d, h * d)


def kernel(x, norm_g, w_in, conv_a_w, conv_r_w, conv_r_b, lru_wa, lru_ba, lru_wx, lru_bx,
           lru_lambda, gmlp_norm_g, gmlp_ws, gmlp_bs, w_out, final_g):
    batch, seq, _ = x.shape
    depth = w_in.shape[0]
    x2 = x.reshape(batch * seq, D_MODEL)
    row = lambda v: v.reshape(1, -1)
    prev = None
    for l in range(depth):
        prm = [
            conv_a_w[l], conv_r_w[l], row(conv_r_b[l]),
            _block_diag(lru_wa[l]).astype(BF16), row(lru_ba[l]),
            _block_diag(lru_wx[l]).astype(BF16), row(lru_bx[l]),
            row(lru_lambda[l]), row(gmlp_norm_g[l]),
            jnp.transpose(gmlp_ws[l], (1, 0, 2)).reshape(GMLP_CHUNK, HEADS * GMLP_CHUNK),
            jnp.repeat(jnp.transpose(gmlp_bs[l]), HEAD_DIM, axis=1),
        ]
        outs = _layer(batch, seq, l, x2, prev, row(norm_g[l]), w_in, w_out, prm)
        if prev is not None:
            x2 = outs[0]
        zq, y_abc = outs[-2:]
        prev = (y_abc, _attention(zq, batch, seq))
    out = _out_final(x2, *prev, w_out, depth - 1, row(final_g))
    return out.reshape(batch, seq, D_MODEL)
```

```python
import functools
import math

import jax
import jax.numpy as jnp
from jax import lax
from jax.experimental import pallas as pl
from jax.experimental.pallas import tpu as pltpu

F32 = jnp.float32
BF16 = jnp.bfloat16

D_MODEL = 1024
GROUP_W = 256
HEADS = 4
HEAD_DIM = 64
N_CHUNKS = 13
N_MIX_CHUNKS = 9
D_IN = N_CHUNKS * GROUP_W
D_MIX_ABC = 3 * GROUP_W
D_ATTN = (N_CHUNKS - N_MIX_CHUNKS) * GROUP_W
RG_C = 8.0
GMLP_CHUNK = 128
ATTN_BLOCK = 128
LANES = 128
ATTN_DILATIONS = (1, 4, 16)
ATTN_TILE = ATTN_BLOCK * 16
GROUP_STRIDE16 = 16
SUBTILES_PER_GROUP = 4
SCORE_LOOKAHEAD = 2
NORM_EPS = 1e-6
MASKED = -1e30
LOG2_E = math.log2(math.e)
Q_SCALE = LOG2_E / math.sqrt(HEAD_DIM)

LAYER_ROWS = 512
RESIDUES = 4
SUBTILE = LAYER_ROWS
assert ATTN_TILE % SUBTILE == 0 and SUBTILE == 4 * ATTN_BLOCK
FINAL_ROWS = 2 * SUBTILE
SUBLANES = 8
CONV_PAD = SUBLANES
VMEM_LIMIT = 56 * 1024 * 1024


def _rms(x, g):
    return x * lax.rsqrt(jnp.mean(x * x, axis=-1, keepdims=True) + NORM_EPS) * g


def _lower_head_mask(rows):
    return lax.broadcasted_iota(jnp.int32, (rows, LANES), 1) < HEAD_DIM


def _only_head(x, hh):
    lower_head = _lower_head_mask(x.shape[0])
    half, lower = hh // 2, hh % 2 == 0
    part = x[:, half * LANES:(half + 1) * LANES]
    part = jnp.where(lower_head, part, 0.0) if lower else jnp.where(lower_head, 0.0, part)
    zeros = jnp.zeros_like(part)
    return jnp.concatenate([part if i == half else zeros for i in range(HEADS // 2)], axis=1)


def _to_residue_major(buf, x):
    per = x.shape[0] // RESIDUES
    for half in range(GROUP_W // LANES):
        buf[half] = x[:, half * LANES:(half + 1) * LANES]
    return jnp.concatenate(
        [jnp.concatenate([buf[half, pl.ds(b, per, stride=RESIDUES), :]
                          for half in range(GROUP_W // LANES)], axis=1)
         for b in range(RESIDUES)], axis=0)


def _from_residue_major(buf, x):
    per = SUBTILE // RESIDUES
    for start in range(0, x.shape[0], SUBTILE):
        for b in range(RESIDUES):
            for half in range(GROUP_W // LANES):
                buf[half, pl.ds(start + b, per, stride=RESIDUES), :] = (
                    x[start + b * per:start + (b + 1) * per, half * LANES:(half + 1) * LANES])
    return jnp.concatenate([buf[half] for half in range(GROUP_W // LANES)], axis=1)


def _causal_conv(carry_ref, x, w_ref):
    rows = x.shape[0]
    taps = w_ref.shape[0]
    history = carry_ref[...]
    carry_ref[...] = x[rows - CONV_PAD:, :]
    row = lax.broadcasted_iota(jnp.int32, history.shape, 0)
    out = x * w_ref[taps - 1:taps, :]
    for k in range(taps - 1):
        shift = taps - 1 - k
        rolled = pltpu.roll(x, shift, axis=0)
        head = jnp.where(row < shift, pltpu.roll(history, shift, axis=0), rolled[0:CONV_PAD])
        shifted = jnp.concatenate([head, rolled[CONV_PAD:]], axis=0)
        out += shifted * w_ref[k:k + 1, :]
    return out


def _linear_scan(a, b, h0):
    rows, cols = a.shape
    groups = rows // SUBLANES
    a = a.reshape(groups, SUBLANES, cols)
    b = b.reshape(groups, SUBLANES, cols)
    row = lax.broadcasted_iota(jnp.int32, a.shape, 1)
    d = 1
    while d < SUBLANES:
        keep = row >= d
        a_prev = jnp.where(keep, pltpu.roll(a, d, axis=1), 1.0)
        b_prev = jnp.where(keep, pltpu.roll(b, d, axis=1), 0.0)
        b = a * b_prev + b
        a = a * a_prev
        d *= 2
    out = []
    for g in range(groups):
        hg = b[g] + a[g] * h0
        out.append(hg)
        h0 = hg[SUBLANES - 1:SUBLANES, :]
    return jnp.concatenate(out, axis=0)


def _mixers_abc(z, other_matmuls, prm, y_ref, carry_a, carry_r, carry_h):
    (conv_a_w_ref, conv_r_w_ref, conv_r_b_ref, wa_ref, ba_ref, wx_ref, bx_ref,
     lam_ref, gmlp_g_ref, ws_ref, bs_ref) = prm
    r_x = z(4)
    rows = r_x.shape[0]
    xb = _causal_conv(carry_r, r_x, conv_r_w_ref) + conv_r_b_ref[...]
    xb16 = xb.astype(BF16)
    for emit in other_matmuls[:OTHER_BEFORE_GATES]:
        emit()
    r = jax.nn.sigmoid(jnp.dot(xb16, wa_ref[...], preferred_element_type=F32) + ba_ref[...])
    i = jax.nn.sigmoid(jnp.dot(xb16, wx_ref[...], preferred_element_type=F32) + bx_ref[...])
    for emit in other_matmuls[OTHER_BEFORE_GATES:]:
        emit()
    a_x, a_b, a_c, a_g = z(0), z(1), z(2), z(3)
    c_v, c_u, c_g = z(7), z(6), z(8)
    r_g = z(5)

    conv_a = _causal_conv(carry_a, a_c * a_x, conv_a_w_ref)
    y_ref[:, 0:GROUP_W] = (a_b * conv_a * jax.nn.silu(a_g)).astype(y_ref.dtype)

    log_a = (-RG_C * r) * jax.nn.softplus(-lam_ref[...])
    a = jnp.exp(log_a)
    th = jnp.tanh(log_a)
    mult = jnp.sqrt(-2.0 * th / (1.0 - th))
    h = _linear_scan(a, mult * (i * xb), carry_h[0:1, :])
    carry_h[...] = jnp.broadcast_to(h[rows - 1:rows, :], carry_h.shape)
    y_ref[:, GROUP_W:2 * GROUP_W] = (h * jax.nn.silu(r_g)).astype(y_ref.dtype)

    vv = _rms(jax.nn.gelu(c_v), gmlp_g_ref[...])
    t_idx = lax.broadcasted_iota(jnp.int32, ws_ref.shape, 0)
    s_idx = lax.broadcasted_iota(jnp.int32, ws_ref.shape, 1) % GMLP_CHUNK
    ws = jnp.where(s_idx <= t_idx, ws_ref[...], 0.0).astype(BF16)
    for j in range(rows // GMLP_CHUNK):
        sl = slice(j * GMLP_CHUNK, (j + 1) * GMLP_CHUNK)
        vj = vv[sl, :]
        v_heads = jnp.concatenate([_only_head(vj, hh) for hh in range(HEADS)], axis=0).astype(BF16)
        sp = jnp.dot(ws, v_heads, preferred_element_type=F32) + bs_ref[...]
        yc = jax.nn.gelu(c_u[sl, :]) * sp * jax.nn.silu(c_g[sl, :])
        y_ref[sl, 2 * GROUP_W:3 * GROUP_W] = yc.astype(y_ref.dtype)


N_MIX_PARAMS = 11
OTHER_BEFORE_GATES = 4


def _residual(x_ref, yabc_ref, yd_ref, w1_ref, w2_ref, perm_buf):
    y_d = _from_residue_major(perm_buf, yd_ref[...].astype(F32)).astype(BF16)
    acc = jnp.dot(yabc_ref[...], w1_ref[...], preferred_element_type=F32)
    acc += jnp.dot(y_d, w2_ref[...], preferred_element_type=F32)
    return x_ref[...] + acc


def _layer_kernel(has_residual, *refs):
    refs = list(refs)
    take = lambda n: [refs.pop(0) for _ in range(n)]
    if has_residual:
        x_ref, yabc_ref, yd_ref, w1_f32, w2_f32 = take(5)
    else:
        (x_ref,) = take(1)
    g_ref, w_f32 = take(2)
    prm = take(N_MIX_PARAMS)
    if has_residual:
        (xo_ref,) = take(1)
    zq_ref, y_ref, carry_a, carry_r, carry_h, perm_buf, w_ref = refs[:7]
    if has_residual:
        w1_ref, w2_ref = refs[7:]

    @pl.when((pl.program_id(0) == 0) & (pl.program_id(1) == 0))
    def _():
        w_ref[...] = w_f32[...].astype(BF16)
        if has_residual:
            w1_ref[...] = w1_f32[...].astype(BF16)
            w2_ref[...] = w2_f32[...].astype(BF16)

    @pl.when(pl.program_id(1) == 0)
    def _():
        carry_a[...] = jnp.zeros_like(carry_a)
        carry_r[...] = jnp.zeros_like(carry_r)
        carry_h[...] = jnp.zeros_like(carry_h)

    if has_residual:
        x = _residual(x_ref, yabc_ref, yd_ref, w1_ref, w2_ref, perm_buf.at[N_CHUNKS - N_MIX_CHUNKS])
        xo_ref[...] = x
    else:
        x = x_ref[...]
    h = _rms(x, g_ref[...]).astype(BF16)

    def z(c):
        cols = slice(c * GROUP_W, (c + 1) * GROUP_W)
        return jnp.dot(h, w_ref[:, cols], preferred_element_type=F32)

    def attention_chunk(k):
        zc = z(N_MIX_CHUNKS + k) * Q_SCALE if k == 0 else z(N_MIX_CHUNKS + k)
        zq_ref[:, k * GROUP_W:(k + 1) * GROUP_W] = _to_residue_major(perm_buf.at[k], zc)

    other = [functools.partial(attention_chunk, k) for k in range(N_CHUNKS - N_MIX_CHUNKS)]
    _mixers_abc(z, other, prm, y_ref, carry_a, carry_r, carry_h)


def _resident(a, l):
    return pl.BlockSpec((None,) + a.shape[1:], lambda *_: (l,) + (0,) * (a.ndim - 1),
                        pipeline_mode=pl.Buffered(1))


def _w_out_specs(w_out, l):
    assert D_MIX_ABC % GROUP_W == 0
    return [pl.BlockSpec((None, D_MIX_ABC, D_MODEL), lambda *_: (l, 0, 0),
                         pipeline_mode=pl.Buffered(1)),
            pl.BlockSpec((None, GROUP_W, D_MODEL), lambda *_: (l, D_MIX_ABC // GROUP_W, 0),
                         pipeline_mode=pl.Buffered(1))]


W_OUT_SCRATCH = [pltpu.VMEM((D_MIX_ABC, D_MODEL), BF16), pltpu.VMEM((GROUP_W, D_MODEL), BF16)]


def _layer(batch, seq, l, x2, prev, g, w_in, w_out, prm):
    tiles = seq // LAYER_ROWS
    n = batch * seq
    has_residual = prev is not None

    def rows(width):
        return pl.BlockSpec((LAYER_ROWS, width), lambda b, t: (b * tiles + t, 0))

    def full(a):
        return pl.BlockSpec(a.shape, lambda b, t: (0,) * a.ndim)

    args = [x2] + (list(prev) + [w_out, w_out] if has_residual else []) + [g, w_in] + list(prm)
    in_specs = [rows(D_MODEL)]
    if has_residual:
        in_specs += [rows(D_MIX_ABC), rows(GROUP_W)] + _w_out_specs(w_out, l - 1)
    in_specs += [full(g), _resident(w_in, l)] + [full(a) for a in prm]
    out_specs = [rows(D_ATTN), rows(D_MIX_ABC)]
    out_shape = [jax.ShapeDtypeStruct((n, D_ATTN), F32),
                 jax.ShapeDtypeStruct((n, D_MIX_ABC), BF16)]
    if has_residual:
        out_specs = [rows(D_MODEL)] + out_specs
        out_shape = [jax.ShapeDtypeStruct((n, D_MODEL), F32)] + out_shape
    return pl.pallas_call(
        functools.partial(_layer_kernel, has_residual),
        grid=(batch, tiles),
        in_specs=in_specs,
        out_specs=out_specs,
        out_shape=out_shape,
        scratch_shapes=[
            pltpu.VMEM((CONV_PAD, GROUP_W), F32),
            pltpu.VMEM((CONV_PAD, GROUP_W), F32),
            pltpu.VMEM((SUBLANES, GROUP_W), F32),
            pltpu.VMEM((N_CHUNKS - N_MIX_CHUNKS + 1, GROUP_W // LANES, LAYER_ROWS, LANES), F32),
            pltpu.VMEM((D_MODEL, D_IN), BF16),
        ] + (W_OUT_SCRATCH if has_residual else []),
        compiler_params=pltpu.CompilerParams(
            dimension_semantics=("arbitrary", "arbitrary"), vmem_limit_bytes=VMEM_LIMIT),
        name="layer_res" if has_residual else "layer_first",
    )(*args)


def _out_final_kernel(x_ref, yabc_ref, yd_ref, w1_f32, w2_f32, g_ref, o_ref,
                      perm_buf, w1_ref, w2_ref):
    @pl.when(pl.program_id(0) == 0)
    def _():
        w1_ref[...] = w1_f32[...].astype(BF16)
        w2_ref[...] = w2_f32[...].astype(BF16)

    o_ref[...] = _rms(_residual(x_ref, yabc_ref, yd_ref, w1_ref, w2_ref, perm_buf), g_ref[...])


def _out_final(x2, y_abc, y_d, w_out, l, g):
    n = x2.shape[0]
    rows = lambda width: pl.BlockSpec((FINAL_ROWS, width), lambda i: (i, 0))
    full = lambda a: pl.BlockSpec(a.shape, lambda i: (0,) * a.ndim)
    return pl.pallas_call(
        _out_final_kernel,
        grid=(n // FINAL_ROWS,),
        in_specs=[rows(D_MODEL), rows(D_MIX_ABC), rows(GROUP_W)] + _w_out_specs(w_out, l) + [full(g)],
        out_specs=rows(D_MODEL),
        out_shape=jax.ShapeDtypeStruct((n, D_MODEL), F32),
        scratch_shapes=[pltpu.VMEM((GROUP_W // LANES, FINAL_ROWS, LANES), F32)] + W_OUT_SCRATCH,
        compiler_params=pltpu.CompilerParams(
            dimension_semantics=("arbitrary",), vmem_limit_bytes=VMEM_LIMIT),
        name="out_final",
    )(x2, y_abc, y_d, w_out, w_out, g)


def _attn_kernel(q0, q1, kp0, kp1, kc0, kc1, vp0, vp1, vc0, vc1, g_ref, y_ref,
                 bias_ref, acc_ref, m_ref, l_ref):
    q_ref, kp_ref, kc_ref, vp_ref, vc_ref = (q0, q1), (kp0, kp1), (kc0, kc1), (vp0, vp1), (vc0, vc1)
    first_tile = (pl.program_id(1) == 0).astype(jnp.int32)
    per = SUBTILE // RESIDUES

    def ld(halves, chunks):
        return jnp.concatenate(
            [jnp.concatenate([h[c, :] for h in halves], axis=1) for c, _ in chunks], axis=0)

    def ld_state(ref, chunks):
        return ld((ref.at[0], ref.at[1]), chunks)

    def st_rows(store, chunks, val):
        start = 0
        for c, size in chunks:
            store(c, val[start:start + size, :])
            start += size

    def st_state(ref, chunks, val):
        def store(c, v):
            ref[0, c, :] = v[:, :LANES]
            ref[1, c, :] = v[:, LANES:]
        st_rows(store, chunks, val)

    @pl.when((pl.program_id(0) == 0) & (pl.program_id(1) == 0))
    def _():
        qi = lax.broadcasted_iota(jnp.int32, (ATTN_BLOCK, 2 * ATTN_BLOCK), 0)
        kj = lax.broadcasted_iota(jnp.int32, (ATTN_BLOCK, 2 * ATTN_BLOCK), 1)
        q_major = RESIDUES * (qi % (per // RESIDUES)) + qi // (per // RESIDUES)
        k_major = (RESIDUES * (kj % (2 * per // RESIDUES) - per // RESIDUES)
                   + kj // (2 * per // RESIDUES))
        for p, dil in enumerate(ATTN_DILATIONS):
            steps = q_major - k_major if dil == 1 else qi + ATTN_BLOCK - kj
            in_block = k_major >= 0 if dil == 1 else kj >= ATTN_BLOCK
            band = (steps >= 0) & (steps <= ATTN_BLOCK)
            for hh in range(HEADS):
                slope = 2.0 ** (-8.0 * (hh + 1) / HEADS)
                bias = (-slope * LOG2_E) * (steps * dil).astype(F32)
                bias_ref[(p * HEADS + hh) * 2] = jnp.where(band, bias, MASKED)
                bias_ref[(p * HEADS + hh) * 2 + 1] = jnp.where(band & in_block, bias, MASKED)

    lower_head = _lower_head_mask(ATTN_BLOCK)

    def expand(cols):
        return jnp.concatenate(
            [jnp.where(lower_head, cols[2 * i], cols[2 * i + 1]) for i in range(HEADS // 2)], axis=1)

    def pick_heads(per_head):
        return jnp.concatenate(
            [jnp.where(lower_head, per_head[2 * i][:, i * LANES:(i + 1) * LANES],
                       per_head[2 * i + 1][:, i * LANES:(i + 1) * LANES])
             for i in range(HEADS // 2)], axis=1)

    def run_blocks(mode, specs):
        def scores_of(spec):
            p, first, rows, kk, vv = spec
            q = ld(q_ref, rows)
            q_heads = jnp.concatenate([_only_head(q, hh) for hh in range(HEADS)],
                                      axis=0).astype(BF16)
            return lax.dot_general(q_heads, kk.astype(BF16), (((1,), (1,)), ((), ())),
                                   preferred_element_type=F32)

        def softmax_of(spec, s_all):
            p, first = spec[:2]
            probs, m_cols, l_cols = [], [], []
            for hh in range(HEADS):
                s = s_all[hh * ATTN_BLOCK:(hh + 1) * ATTN_BLOCK, :]
                s = s + bias_ref[(p * HEADS + hh) * 2 + first]
                m_h = jnp.max(s, axis=-1, keepdims=True)
                e = jnp.exp2(s - m_h)
                probs.append(e.astype(BF16))
                m_cols.append(m_h)
                l_cols.append(jnp.sum(e, axis=-1, keepdims=True))
            return jnp.concatenate(probs, axis=0), m_cols, l_cols

        def values_of(spec, p_all, m_cols, l_cols):
            p, first, rows, kk, vv = spec
            o_all = jnp.dot(p_all, vv.astype(BF16), preferred_element_type=F32)
            acc = pick_heads([o_all[hh * ATTN_BLOCK:(hh + 1) * ATTN_BLOCK, :] for hh in range(HEADS)])
            m_b = expand(m_cols)
            l_b = expand(l_cols)
            if mode != "init":
                m_s, l_s, acc_s = ld_state(m_ref, rows), ld_state(l_ref, rows), ld_state(acc_ref, rows)
                m_n = jnp.maximum(m_s, m_b)
                w_s = jnp.exp2(m_s - m_n)
                w_b = jnp.exp2(m_b - m_n)
                acc = acc_s * w_s + acc * w_b
                l_b = l_s * w_s + l_b * w_b
                m_b = m_n
            if mode == "final":
                gate = jnp.concatenate([g_ref[c, :] for c, _ in rows], axis=0)
                out = (acc / l_b * jax.nn.silu(gate)).astype(y_ref.dtype)

                def store(c, v):
                    y_ref[c, :] = v
                st_rows(store, rows, out)
            else:
                st_state(m_ref, rows, m_b)
                st_state(l_ref, rows, l_b)
                st_state(acc_ref, rows, acc)

        scores = {i: scores_of(specs[i]) for i in range(min(SCORE_LOOKAHEAD, len(specs)))}
        for i, spec in enumerate(specs):
            soft = softmax_of(spec, scores.pop(i))
            if i + SCORE_LOOKAHEAD < len(specs):
                scores[i + SCORE_LOOKAHEAD] = scores_of(specs[i + SCORE_LOOKAHEAD])
            values_of(spec, *soft)

    def keys_values(prev_refs, prev_rows, rows):
        k_src, v_src = (kp_ref, vp_ref) if prev_refs else (kc_ref, vc_ref)
        kk = jnp.concatenate([ld(k_src, prev_rows), ld(kc_ref, rows)], axis=0)
        vv = jnp.concatenate([ld(v_src, prev_rows), ld(vc_ref, rows)], axis=0)
        return kk, vv


    def spec16(r):
        a, b = r // RESIDUES, r % RESIDUES
        rows = [(pl.ds(t * SUBTILE + b * per + a, per // RESIDUES, stride=RESIDUES), per // RESIDUES)
                for t in range(ATTN_TILE // SUBTILE)]
        return (2, first_tile, rows) + keys_values(1, rows, rows)

    def body16(i, carry):
        run_blocks("init", [spec16(GROUP_STRIDE16 * i + j) for j in range(GROUP_STRIDE16)])
        return carry
    lax.fori_loop(0, 16 // GROUP_STRIDE16, body16, 0)

    def spec4(t, b):
        rows = [(pl.ds(t * SUBTILE + b * per, per), per)]
        t_prev = (t - 1) % (ATTN_TILE // SUBTILE)
        prev_rows = [(pl.ds(t_prev * SUBTILE + b * per, per), per)]
        first = first_tile if t == 0 else 0
        return (1, first, rows) + keys_values(t == 0, prev_rows, rows)

    for t0 in range(0, ATTN_TILE // SUBTILE, SUBTILES_PER_GROUP):
        run_blocks("merge", [spec4(t, b) for t in range(t0, t0 + SUBTILES_PER_GROUP)
                             for b in range(RESIDUES)])

    n_sub = SUBTILE // ATTN_BLOCK
    blk = per // n_sub

    def spec1(t, u):
        base = t * SUBTILE
        rows = [(pl.ds(base + b * per + u * blk, blk), blk) for b in range(RESIDUES)]
        if u > 0:
            key_rows = [(pl.ds(base + b * per + (u - 1) * blk, 2 * blk), 2 * blk)
                        for b in range(RESIDUES)]
            return (0, 0, rows, ld(kc_ref, key_rows), ld(vc_ref, key_rows))
        first_of_tile = t == 0
        k_src, v_src = (kp_ref, vp_ref) if first_of_tile else (kc_ref, vc_ref)
        prev_base = (ATTN_TILE - SUBTILE) if first_of_tile else base - SUBTILE
        kk, vv = [], []
        for b in range(RESIDUES):
            before = [(pl.ds(prev_base + b * per + per - blk, blk), blk)]
            kk += [ld(k_src, before), ld(kc_ref, rows[b:b + 1])]
            vv += [ld(v_src, before), ld(vc_ref, rows[b:b + 1])]
        first = first_tile if first_of_tile else 0
        return (0, first, rows, jnp.concatenate(kk, axis=0), jnp.concatenate(vv, axis=0))

    for t0 in range(0, ATTN_TILE // SUBTILE, SUBTILES_PER_GROUP):
        run_blocks("final", [spec1(t, u) for t in range(t0, t0 + SUBTILES_PER_GROUP)
                             for u in range(n_sub)])


def _attention(zq, batch, seq):
    tiles = seq // ATTN_TILE
    halves = GROUP_W // LANES

    def cur(c):
        return [pl.BlockSpec((ATTN_TILE, LANES), lambda b, n, c=c, h=h: (b * tiles + n, c * halves + h))
                for h in range(halves)]

    def prev(c):
        return [pl.BlockSpec((ATTN_TILE, LANES),
                             lambda b, n, c=c, h=h: (b * tiles + jnp.maximum(n - 1, 0), c * halves + h))
                for h in range(halves)]

    in_specs = cur(0) + prev(1) + cur(1) + prev(2) + cur(2) + [
        pl.BlockSpec((ATTN_TILE, GROUP_W), lambda b, n: (b * tiles + n, 3))]
    state = pltpu.VMEM((halves, ATTN_TILE, LANES), F32)
    return pl.pallas_call(
        _attn_kernel,
        grid=(batch, tiles),
        in_specs=in_specs,
        out_specs=pl.BlockSpec((ATTN_TILE, GROUP_W), lambda b, n: (b * tiles + n, 0)),
        out_shape=jax.ShapeDtypeStruct((batch * seq, GROUP_W), BF16),
        scratch_shapes=[
            pltpu.VMEM((len(ATTN_DILATIONS) * HEADS * 2, ATTN_BLOCK, 2 * ATTN_BLOCK), F32),
            state, state, state,
        ],
        compiler_params=pltpu.CompilerParams(
            dimension_semantics=("arbitrary", "arbitrary"), vmem_limit_bytes=VMEM_LIMIT),
        name="dilated_attention",
    )(*([zq] * len(in_specs)))


def _block_diag(w):
    h, d, _ = w.shape
    eye = jnp.eye(h, dtype=w.dtype)
    return jnp.einsum("hij,hg->higj", w, eye).reshape(h * d, h * d)


def kernel(x, norm_g, w_in, conv_a_w, conv_r_w, conv_r_b, lru_wa, lru_ba, lru_wx, lru_bx,
           lru_lambda, gmlp_norm_g, gmlp_ws, gmlp_bs, w_out, final_g):
    batch, seq, _ = x.shape
    depth = w_in.shape[0]
    x2 = x.reshape(batch * seq, D_MODEL)
    row = lambda v: v.reshape(1, -1)
    prev = None
    for l in range(depth):
        prm = [
            conv_a_w[l], conv_r_w[l], row(conv_r_b[l]),
            _block_diag(lru_wa[l]).astype(BF16), row(lru_ba[l]),
            _block_diag(lru_wx[l]).astype(BF16), row(lru_bx[l]),
            row(lru_lambda[l]), row(gmlp_norm_g[l]),
            jnp.transpose(gmlp_ws[l], (1, 0, 2)).reshape(GMLP_CHUNK, HEADS * GMLP_CHUNK),
            jnp.repeat(jnp.transpose(gmlp_bs[l]), HEAD_DIM, axis=1),
        ]
        outs = _layer(batch, seq, l, x2, prev, row(norm_g[l]), w_in, w_out, prm)
        if prev is not None:
            x2 = outs[0]
        zq, y_abc = outs[-2:]
        prev = (y_abc, _attention(zq, batch, seq))
    out = _out_final(x2, *prev, w_out, depth - 1, row(final_g))
    return out.reshape(batch, seq, D_MODEL)
```

```python
import functools
import math

import jax
import jax.numpy as jnp
from jax import lax
from jax.experimental import pallas as pl
from jax.experimental.pallas import tpu as pltpu

F32 = jnp.float32
BF16 = jnp.bfloat16

D_MODEL = 1024
GROUP_W = 256
HEADS = 4
HEAD_DIM = 64
N_CHUNKS = 13
N_MIX_CHUNKS = 9
D_IN = N_CHUNKS * GROUP_W
D_MIX_ABC = 3 * GROUP_W
D_ATTN = (N_CHUNKS - N_MIX_CHUNKS) * GROUP_W
RG_C = 8.0
GMLP_CHUNK = 128
ATTN_BLOCK = 128
LANES = 128
ATTN_DILATIONS = (1, 4, 16)
ATTN_TILE = ATTN_BLOCK * 16
GROUP_STRIDE16 = 16
SUBTILES_PER_GROUP = 4
SCORE_LOOKAHEAD = 1
NORM_EPS = 1e-6
MASKED = -1e30
LOG2_E = math.log2(math.e)
Q_SCALE = LOG2_E / math.sqrt(HEAD_DIM)

LAYER_ROWS = 512
RESIDUES = 4
SUBTILE = LAYER_ROWS
assert ATTN_TILE % SUBTILE == 0 and SUBTILE == 4 * ATTN_BLOCK
FINAL_ROWS = 2 * SUBTILE
SUBLANES = 8
CONV_PAD = SUBLANES
VMEM_LIMIT = 56 * 1024 * 1024


def _rms(x, g):
    return x * lax.rsqrt(jnp.mean(x * x, axis=-1, keepdims=True) + NORM_EPS) * g


def _lower_head_mask(rows):
    return lax.broadcasted_iota(jnp.int32, (rows, LANES), 1) < HEAD_DIM


def _only_head(x, hh):
    lower_head = _lower_head_mask(x.shape[0])
    half, lower = hh // 2, hh % 2 == 0
    part = x[:, half * LANES:(half + 1) * LANES]
    part = jnp.where(lower_head, part, 0.0) if lower else jnp.where(lower_head, 0.0, part)
    zeros = jnp.zeros_like(part)
    return jnp.concatenate([part if i == half else zeros for i in range(HEADS // 2)], axis=1)


def _to_residue_major(buf, x):
    per = x.shape[0] // RESIDUES
    for half in range(GROUP_W // LANES):
        buf[half] = x[:, half * LANES:(half + 1) * LANES]
    return jnp.concatenate(
        [jnp.concatenate([buf[half, pl.ds(b, per, stride=RESIDUES), :]
                          for half in range(GROUP_W // LANES)], axis=1)
         for b in range(RESIDUES)], axis=0)


def _from_residue_major(buf, x):
    per = SUBTILE // RESIDUES
    for start in range(0, x.shape[0], SUBTILE):
        for b in range(RESIDUES):
            for half in range(GROUP_W // LANES):
                buf[half, pl.ds(start + b, per, stride=RESIDUES), :] = (
                    x[start + b * per:start + (b + 1) * per, half * LANES:(half + 1) * LANES])
    return jnp.concatenate([buf[half] for half in range(GROUP_W // LANES)], axis=1)


def _causal_conv(carry_ref, x, w_ref):
    rows = x.shape[0]
    taps = w_ref.shape[0]
    history = carry_ref[...]
    carry_ref[...] = x[rows - CONV_PAD:, :]
    row = lax.broadcasted_iota(jnp.int32, history.shape, 0)
    out = x * w_ref[taps - 1:taps, :]
    for k in range(taps - 1):
        shift = taps - 1 - k
        rolled = pltpu.roll(x, shift, axis=0)
        head = jnp.where(row < shift, pltpu.roll(history, shift, axis=0), rolled[0:CONV_PAD])
        shifted = jnp.concatenate([head, rolled[CONV_PAD:]], axis=0)
        out += shifted * w_ref[k:k + 1, :]
    return out


def _linear_scan(a, b, h0):
    rows, cols = a.shape
    groups = rows // SUBLANES
    a = a.reshape(groups, SUBLANES, cols)
    b = b.reshape(groups, SUBLANES, cols)
    row = lax.broadcasted_iota(jnp.int32, a.shape, 1)
    d = 1
    while d < SUBLANES:
        keep = row >= d
        a_prev = jnp.where(keep, pltpu.roll(a, d, axis=1), 1.0)
        b_prev = jnp.where(keep, pltpu.roll(b, d, axis=1), 0.0)
        b = a * b_prev + b
        a = a * a_prev
        d *= 2
    out = []
    for g in range(groups):
        hg = b[g] + a[g] * h0
        out.append(hg)
        h0 = hg[SUBLANES - 1:SUBLANES, :]
    return jnp.concatenate(out, axis=0)


def _mixers_abc(z, emit_other_matmuls, prm, y_ref, carry_a, carry_r, carry_h):
    (conv_a_w_ref, conv_r_w_ref, conv_r_b_ref, wa_ref, ba_ref, wx_ref, bx_ref,
     lam_ref, gmlp_g_ref, ws_ref, bs_ref) = prm
    r_x = z(4)
    rows = r_x.shape[0]
    xb = _causal_conv(carry_r, r_x, conv_r_w_ref) + conv_r_b_ref[...]
    xb16 = xb.astype(BF16)
    a_x, a_b, a_c, a_g = z(0), z(1), z(2), z(3)
    r = jax.nn.sigmoid(jnp.dot(xb16, wa_ref[...], preferred_element_type=F32) + ba_ref[...])
    i = jax.nn.sigmoid(jnp.dot(xb16, wx_ref[...], preferred_element_type=F32) + bx_ref[...])
    c_v, c_u, c_g = z(7), z(6), z(8)
    r_g = z(5)
    emit_other_matmuls()

    conv_a = _causal_conv(carry_a, a_c * a_x, conv_a_w_ref)
    y_ref[:, 0:GROUP_W] = (a_b * conv_a * jax.nn.silu(a_g)).astype(y_ref.dtype)

    log_a = (-RG_C * r) * jax.nn.softplus(-lam_ref[...])
    a = jnp.exp(log_a)
    th = jnp.tanh(log_a)
    mult = jnp.sqrt(-2.0 * th / (1.0 - th))
    h = _linear_scan(a, mult * (i * xb), carry_h[0:1, :])
    carry_h[...] = jnp.broadcast_to(h[rows - 1:rows, :], carry_h.shape)
    y_ref[:, GROUP_W:2 * GROUP_W] = (h * jax.nn.silu(r_g)).astype(y_ref.dtype)

    vv = _rms(jax.nn.gelu(c_v), gmlp_g_ref[...])
    t_idx = lax.broadcasted_iota(jnp.int32, ws_ref.shape, 0)
    s_idx = lax.broadcasted_iota(jnp.int32, ws_ref.shape, 1) % GMLP_CHUNK
    ws = jnp.where(s_idx <= t_idx, ws_ref[...], 0.0).astype(BF16)
    for j in range(rows // GMLP_CHUNK):
        sl = slice(j * GMLP_CHUNK, (j + 1) * GMLP_CHUNK)
        vj = vv[sl, :]
        v_heads = jnp.concatenate([_only_head(vj, hh) for hh in range(HEADS)], axis=0).astype(BF16)
        sp = jnp.dot(ws, v_heads, preferred_element_type=F32) + bs_ref[...]
        yc = jax.nn.gelu(c_u[sl, :]) * sp * jax.nn.silu(c_g[sl, :])
        y_ref[sl, 2 * GROUP_W:3 * GROUP_W] = yc.astype(y_ref.dtype)


N_MIX_PARAMS = 11


def _residual(x_ref, yabc_ref, yd_ref, w1_ref, w2_ref, perm_buf):
    y_d = _from_residue_major(perm_buf, yd_ref[...].astype(F32)).astype(BF16)
    y = jnp.concatenate([yabc_ref[...], y_d], axis=1)
    w = jnp.concatenate([w1_ref[...], w2_ref[...]], axis=0)
    return x_ref[...] + jnp.dot(y, w, preferred_element_type=F32)


def _layer_kernel(has_residual, *refs):
    refs = list(refs)
    take = lambda n: [refs.pop(0) for _ in range(n)]
    if has_residual:
        x_ref, yabc_ref, yd_ref, w1_f32, w2_f32 = take(5)
    else:
        (x_ref,) = take(1)
    g_ref, w_f32 = take(2)
    prm = take(N_MIX_PARAMS)
    if has_residual:
        (xo_ref,) = take(1)
    zq_ref, y_ref, carry_a, carry_r, carry_h, perm_buf, w_ref = refs[:7]
    if has_residual:
        w1_ref, w2_ref = refs[7:]

    @pl.when((pl.program_id(0) == 0) & (pl.program_id(1) == 0))
    def _():
        w_ref[...] = w_f32[...].astype(BF16)
        if has_residual:
            w1_ref[...] = w1_f32[...].astype(BF16)
            w2_ref[...] = w2_f32[...].astype(BF16)

    @pl.when(pl.program_id(1) == 0)
    def _():
        carry_a[...] = jnp.zeros_like(carry_a)
        carry_r[...] = jnp.zeros_like(carry_r)
        carry_h[...] = jnp.zeros_like(carry_h)

    if has_residual:
        x = _residual(x_ref, yabc_ref, yd_ref, w1_ref, w2_ref, perm_buf.at[N_CHUNKS - N_MIX_CHUNKS])
        xo_ref[...] = x
    else:
        x = x_ref[...]
    h = _rms(x, g_ref[...]).astype(BF16)

    def z(c):
        cols = slice(c * GROUP_W, (c + 1) * GROUP_W)
        return jnp.dot(h, w_ref[:, cols], preferred_element_type=F32)

    def attention_chunks():
        for k, c in enumerate(range(N_MIX_CHUNKS, N_CHUNKS)):
            zc = z(c) * Q_SCALE if k == 0 else z(c)
            zq_ref[:, k * GROUP_W:(k + 1) * GROUP_W] = _to_residue_major(perm_buf.at[k], zc)

    _mixers_abc(z, attention_chunks, prm, y_ref, carry_a, carry_r, carry_h)


def _resident(a, l):
    return pl.BlockSpec((None,) + a.shape[1:], lambda *_: (l,) + (0,) * (a.ndim - 1),
                        pipeline_mode=pl.Buffered(1))


def _w_out_specs(w_out, l):
    assert D_MIX_ABC % GROUP_W == 0
    return [pl.BlockSpec((None, D_MIX_ABC, D_MODEL), lambda *_: (l, 0, 0),
                         pipeline_mode=pl.Buffered(1)),
            pl.BlockSpec((None, GROUP_W, D_MODEL), lambda *_: (l, D_MIX_ABC // GROUP_W, 0),
                         pipeline_mode=pl.Buffered(1))]


W_OUT_SCRATCH = [pltpu.VMEM((D_MIX_ABC, D_MODEL), BF16), pltpu.VMEM((GROUP_W, D_MODEL), BF16)]


def _layer(batch, seq, l, x2, prev, g, w_in, w_out, prm):
    tiles = seq // LAYER_ROWS
    n = batch * seq
    has_residual = prev is not None

    def rows(width):
        return pl.BlockSpec((LAYER_ROWS, width), lambda b, t: (b * tiles + t, 0))

    def full(a):
        return pl.BlockSpec(a.shape, lambda b, t: (0,) * a.ndim)

    args = [x2] + (list(prev) + [w_out, w_out] if has_residual else []) + [g, w_in] + list(prm)
    in_specs = [rows(D_MODEL)]
    if has_residual:
        in_specs += [rows(D_MIX_ABC), rows(GROUP_W)] + _w_out_specs(w_out, l - 1)
    in_specs += [full(g), _resident(w_in, l)] + [full(a) for a in prm]
    out_specs = [rows(D_ATTN), rows(D_MIX_ABC)]
    out_shape = [jax.ShapeDtypeStruct((n, D_ATTN), F32),
                 jax.ShapeDtypeStruct((n, D_MIX_ABC), BF16)]
    if has_residual:
        out_specs = [rows(D_MODEL)] + out_specs
        out_shape = [jax.ShapeDtypeStruct((n, D_MODEL), F32)] + out_shape
    return pl.pallas_call(
        functools.partial(_layer_kernel, has_residual),
        grid=(batch, tiles),
        in_specs=in_specs,
        out_specs=out_specs,
        out_shape=out_shape,
        scratch_shapes=[
            pltpu.VMEM((CONV_PAD, GROUP_W), F32),
            pltpu.VMEM((CONV_PAD, GROUP_W), F32),
            pltpu.VMEM((SUBLANES, GROUP_W), F32),
            pltpu.VMEM((N_CHUNKS - N_MIX_CHUNKS + 1, GROUP_W // LANES, LAYER_ROWS, LANES), F32),
            pltpu.VMEM((D_MODEL, D_IN), BF16),
        ] + (W_OUT_SCRATCH if has_residual else []),
        compiler_params=pltpu.CompilerParams(
            dimension_semantics=("arbitrary", "arbitrary"), vmem_limit_bytes=VMEM_LIMIT),
        name="layer_res" if has_residual else "layer_first",
    )(*args)


def _out_final_kernel(x_ref, yabc_ref, yd_ref, w1_f32, w2_f32, g_ref, o_ref,
                      perm_buf, w1_ref, w2_ref):
    @pl.when(pl.program_id(0) == 0)
    def _():
        w1_ref[...] = w1_f32[...].astype(BF16)
        w2_ref[...] = w2_f32[...].astype(BF16)

    o_ref[...] = _rms(_residual(x_ref, yabc_ref, yd_ref, w1_ref, w2_ref, perm_buf), g_ref[...])


def _out_final(x2, y_abc, y_d, w_out, l, g):
    n = x2.shape[0]
    rows = lambda width: pl.BlockSpec((FINAL_ROWS, width), lambda i: (i, 0))
    full = lambda a: pl.BlockSpec(a.shape, lambda i: (0,) * a.ndim)
    return pl.pallas_call(
        _out_final_kernel,
        grid=(n // FINAL_ROWS,),
        in_specs=[rows(D_MODEL), rows(D_MIX_ABC), rows(GROUP_W)] + _w_out_specs(w_out, l) + [full(g)],
        out_specs=rows(D_MODEL),
        out_shape=jax.ShapeDtypeStruct((n, D_MODEL), F32),
        scratch_shapes=[pltpu.VMEM((GROUP_W // LANES, FINAL_ROWS, LANES), F32)] + W_OUT_SCRATCH,
        compiler_params=pltpu.CompilerParams(
            dimension_semantics=("arbitrary",), vmem_limit_bytes=VMEM_LIMIT),
        name="out_final",
    )(x2, y_abc, y_d, w_out, w_out, g)


def _attn_kernel(q0, q1, kp0, kp1, kc0, kc1, vp0, vp1, vc0, vc1, g_ref, y_ref,
                 bias_ref, acc_ref, m_ref, l_ref):
    q_ref, kp_ref, kc_ref, vp_ref, vc_ref = (q0, q1), (kp0, kp1), (kc0, kc1), (vp0, vp1), (vc0, vc1)
    first_tile = (pl.program_id(1) == 0).astype(jnp.int32)
    per = SUBTILE // RESIDUES

    def ld(halves, chunks):
        return jnp.concatenate(
            [jnp.concatenate([h[c, :] for h in halves], axis=1) for c, _ in chunks], axis=0)

    def ld_state(ref, chunks):
        return ld((ref.at[0], ref.at[1]), chunks)

    def st_rows(store, chunks, val):
        start = 0
        for c, size in chunks:
            store(c, val[start:start + size, :])
            start += size

    def st_state(ref, chunks, val):
        def store(c, v):
            ref[0, c, :] = v[:, :LANES]
            ref[1, c, :] = v[:, LANES:]
        st_rows(store, chunks, val)

    @pl.when((pl.program_id(0) == 0) & (pl.program_id(1) == 0))
    def _():
        qi = lax.broadcasted_iota(jnp.int32, (ATTN_BLOCK, 2 * ATTN_BLOCK), 0)
        kj = lax.broadcasted_iota(jnp.int32, (ATTN_BLOCK, 2 * ATTN_BLOCK), 1)
        q_major = RESIDUES * (qi % (per // RESIDUES)) + qi // (per // RESIDUES)
        k_major = (RESIDUES * (kj % (2 * per // RESIDUES) - per // RESIDUES)
                   + kj // (2 * per // RESIDUES))
        for p, dil in enumerate(ATTN_DILATIONS):
            steps = q_major - k_major if dil == 1 else qi + ATTN_BLOCK - kj
            in_block = k_major >= 0 if dil == 1 else kj >= ATTN_BLOCK
            band = (steps >= 0) & (steps <= ATTN_BLOCK)
            for hh in range(HEADS):
                slope = 2.0 ** (-8.0 * (hh + 1) / HEADS)
                bias = (-slope * LOG2_E) * (steps * dil).astype(F32)
                bias_ref[(p * HEADS + hh) * 2] = jnp.where(band, bias, MASKED)
                bias_ref[(p * HEADS + hh) * 2 + 1] = jnp.where(band & in_block, bias, MASKED)

    lower_head = _lower_head_mask(ATTN_BLOCK)

    def expand(cols):
        return jnp.concatenate(
            [jnp.where(lower_head, cols[2 * i], cols[2 * i + 1]) for i in range(HEADS // 2)], axis=1)

    def pick_heads(per_head):
        return jnp.concatenate(
            [jnp.where(lower_head, per_head[2 * i][:, i * LANES:(i + 1) * LANES],
                       per_head[2 * i + 1][:, i * LANES:(i + 1) * LANES])
             for i in range(HEADS // 2)], axis=1)

    def run_blocks(mode, specs):
        def scores_of(spec):
            p, first, rows, kk, vv = spec
            q = ld(q_ref, rows)
            q_heads = jnp.concatenate([_only_head(q, hh) for hh in range(HEADS)],
                                      axis=0).astype(BF16)
            return lax.dot_general(q_heads, kk.astype(BF16), (((1,), (1,)), ((), ())),
                                   preferred_element_type=F32)

        def softmax_of(spec, s_all):
            p, first = spec[:2]
            probs, m_cols, l_cols = [], [], []
            for hh in range(HEADS):
                s = s_all[hh * ATTN_BLOCK:(hh + 1) * ATTN_BLOCK, :]
                s = s + bias_ref[(p * HEADS + hh) * 2 + first]
                m_h = jnp.max(s, axis=-1, keepdims=True)
                e = jnp.exp2(s - m_h)
                probs.append(e.astype(BF16))
                m_cols.append(m_h)
                l_cols.append(jnp.sum(e, axis=-1, keepdims=True))
            return jnp.concatenate(probs, axis=0), m_cols, l_cols

        def values_of(spec, p_all, m_cols, l_cols):
            p, first, rows, kk, vv = spec
            o_all = jnp.dot(p_all, vv.astype(BF16), preferred_element_type=F32)
            acc = pick_heads([o_all[hh * ATTN_BLOCK:(hh + 1) * ATTN_BLOCK, :] for hh in range(HEADS)])
            m_b = expand(m_cols)
            l_b = expand(l_cols)
            if mode != "init":
                m_s, l_s, acc_s = ld_state(m_ref, rows), ld_state(l_ref, rows), ld_state(acc_ref, rows)
                m_n = jnp.maximum(m_s, m_b)
                w_s = jnp.exp2(m_s - m_n)
                w_b = jnp.exp2(m_b - m_n)
                acc = acc_s * w_s + acc * w_b
                l_b = l_s * w_s + l_b * w_b
                m_b = m_n
            if mode == "final":
                gate = jnp.concatenate([g_ref[c, :] for c, _ in rows], axis=0)
                out = (acc / l_b * jax.nn.silu(gate)).astype(y_ref.dtype)

                def store(c, v):
                    y_ref[c, :] = v
                st_rows(store, rows, out)
            else:
                st_state(m_ref, rows, m_b)
                st_state(l_ref, rows, l_b)
                st_state(acc_ref, rows, acc)

        scores = {i: scores_of(specs[i]) for i in range(min(SCORE_LOOKAHEAD, len(specs)))}
        for i, spec in enumerate(specs):
            soft = softmax_of(spec, scores.pop(i))
            if i + SCORE_LOOKAHEAD < len(specs):
                scores[i + SCORE_LOOKAHEAD] = scores_of(specs[i + SCORE_LOOKAHEAD])
            values_of(spec, *soft)

    def keys_values(prev_refs, prev_rows, rows):
        k_src, v_src = (kp_ref, vp_ref) if prev_refs else (kc_ref, vc_ref)
        kk = jnp.concatenate([ld(k_src, prev_rows), ld(kc_ref, rows)], axis=0)
        vv = jnp.concatenate([ld(v_src, prev_rows), ld(vc_ref, rows)], axis=0)
        return kk, vv


    def spec16(r):
        a, b = r // RESIDUES, r % RESIDUES
        rows = [(pl.ds(t * SUBTILE + b * per + a, per // RESIDUES, stride=RESIDUES), per // RESIDUES)
                for t in range(ATTN_TILE // SUBTILE)]
        return (2, first_tile, rows) + keys_values(1, rows, rows)

    def body16(i, carry):
        run_blocks("init", [spec16(GROUP_STRIDE16 * i + j) for j in range(GROUP_STRIDE16)])
        return carry
    lax.fori_loop(0, 16 // GROUP_STRIDE16, body16, 0)

    def spec4(t, b):
        rows = [(pl.ds(t * SUBTILE + b * per, per), per)]
        t_prev = (t - 1) % (ATTN_TILE // SUBTILE)
        prev_rows = [(pl.ds(t_prev * SUBTILE + b * per, per), per)]
        first = first_tile if t == 0 else 0
        return (1, first, rows) + keys_values(t == 0, prev_rows, rows)

    for t0 in range(0, ATTN_TILE // SUBTILE, SUBTILES_PER_GROUP):
        run_blocks("merge", [spec4(t, b) for t in range(t0, t0 + SUBTILES_PER_GROUP)
                             for b in range(RESIDUES)])

    n_sub = SUBTILE // ATTN_BLOCK
    blk = per // n_sub

    def spec1(t, u):
        base = t * SUBTILE
        rows = [(pl.ds(base + b * per + u * blk, blk), blk) for b in range(RESIDUES)]
        if u > 0:
            key_rows = [(pl.ds(base + b * per + (u - 1) * blk, 2 * blk), 2 * blk)
                        for b in range(RESIDUES)]
            return (0, 0, rows, ld(kc_ref, key_rows), ld(vc_ref, key_rows))
        first_of_tile = t == 0
        k_src, v_src = (kp_ref, vp_ref) if first_of_tile else (kc_ref, vc_ref)
        prev_base = (ATTN_TILE - SUBTILE) if first_of_tile else base - SUBTILE
        kk, vv = [], []
        for b in range(RESIDUES):
            before = [(pl.ds(prev_base + b * per + per - blk, blk), blk)]
            kk += [ld(k_src, before), ld(kc_ref, rows[b:b + 1])]
            vv += [ld(v_src, before), ld(vc_ref, rows[b:b + 1])]
        first = first_tile if first_of_tile else 0
        return (0, first, rows, jnp.concatenate(kk, axis=0), jnp.concatenate(vv, axis=0))

    for t0 in range(0, ATTN_TILE // SUBTILE, SUBTILES_PER_GROUP):
        run_blocks("final", [spec1(t, u) for t in range(t0, t0 + SUBTILES_PER_GROUP)
                             for u in range(n_sub)])


def _attention(zq, batch, seq):
    tiles = seq // ATTN_TILE
    halves = GROUP_W // LANES

    def cur(c):
        return [pl.BlockSpec((ATTN_TILE, LANES), lambda b, n, c=c, h=h: (b * tiles + n, c * halves + h))
                for h in range(halves)]

    def prev(c):
        return [pl.BlockSpec((ATTN_TILE, LANES),
                             lambda b, n, c=c, h=h: (b * tiles + jnp.maximum(n - 1, 0), c * halves + h))
                for h in range(halves)]

    in_specs = cur(0) + prev(1) + cur(1) + prev(2) + cur(2) + [
        pl.BlockSpec((ATTN_TILE, GROUP_W), lambda b, n: (b * tiles + n, 3))]
    state = pltpu.VMEM((halves, ATTN_TILE, LANES), F32)
    return pl.pallas_call(
        _attn_kernel,
        grid=(batch, tiles),
        in_specs=in_specs,
        out_specs=pl.BlockSpec((ATTN_TILE, GROUP_W), lambda b, n: (b * tiles + n, 0)),
        out_shape=jax.ShapeDtypeStruct((batch * seq, GROUP_W), BF16),
        scratch_shapes=[
            pltpu.VMEM((len(ATTN_DILATIONS) * HEADS * 2, ATTN_BLOCK, 2 * ATTN_BLOCK), F32),
            state, state, state,
        ],
        compiler_params=pltpu.CompilerParams(
            dimension_semantics=("arbitrary", "arbitrary"), vmem_limit_bytes=VMEM_LIMIT),
        name="dilated_attention",
    )(*([zq] * len(in_specs)))


def _block_diag(w):
    h, d, _ = w.shape
    eye = jnp.eye(h, dtype=w.dtype)
    return jnp.einsum("hij,hg->higj", w, eye).reshape(h * d, h * d)


def kernel(x, norm_g, w_in, conv_a_w, conv_r_w, conv_r_b, lru_wa, lru_ba, lru_wx, lru_bx,
           lru_lambda, gmlp_norm_g, gmlp_ws, gmlp_bs, w_out, final_g):
    batch, seq, _ = x.shape
    depth = w_in.shape[0]
    x2 = x.reshape(batch * seq, D_MODEL)
    row = lambda v: v.reshape(1, -1)
    prev = None
    for l in range(depth):
        prm = [
            conv_a_w[l], conv_r_w[l], row(conv_r_b[l]),
            _block_diag(lru_wa[l]).astype(BF16), row(lru_ba[l]),
            _block_diag(lru_wx[l]).astype(BF16), row(lru_bx[l]),
            row(lru_lambda[l]), row(gmlp_norm_g[l]),
            jnp.transpose(gmlp_ws[l], (1, 0, 2)).reshape(GMLP_CHUNK, HEADS * GMLP_CHUNK),
            jnp.repeat(jnp.transpose(gmlp_bs[l]), HEAD_DIM, axis=1),
        ]
        outs = _layer(batch, seq, l, x2, prev, row(norm_g[l]), w_in, w_out, prm)
        if prev is not None:
            x2 = outs[0]
        zq, y_abc = outs[-2:]
        prev = (y_abc, _attention(zq, batch, seq))
    out = _out_final(x2, *prev, w_out, depth - 1, row(final_g))
    return out.reshape(batch, seq, D_MODEL)
```

```python
import functools
import math

import jax
import jax.numpy as jnp
from jax import lax
from jax.experimental import pallas as pl
from jax.experimental.pallas import tpu as pltpu

F32 = jnp.float32
BF16 = jnp.bfloat16

D_MODEL = 1024
GROUP_W = 256
HEADS = 4
HEAD_DIM = 64
N_CHUNKS = 13
N_MIX_CHUNKS = 9
D_IN = N_CHUNKS * GROUP_W
D_MIX_ABC = 3 * GROUP_W
D_ATTN = (N_CHUNKS - N_MIX_CHUNKS) * GROUP_W
RG_C = 8.0
GMLP_CHUNK = 128
ATTN_BLOCK = 128
LANES = 128
ATTN_DILATIONS = (1, 4, 16)
ATTN_TILE = ATTN_BLOCK * 16
GROUP_STRIDE16 = 16
SUBTILES_PER_GROUP = 4
SCORE_LOOKAHEAD = 1
NORM_EPS = 1e-6
MASKED = -1e30
LOG2_E = math.log2(math.e)
Q_SCALE = LOG2_E / math.sqrt(HEAD_DIM)

LAYER_ROWS = 512
RESIDUES = 4
SUBTILE = LAYER_ROWS
assert ATTN_TILE % SUBTILE == 0 and SUBTILE == 4 * ATTN_BLOCK
FINAL_ROWS = 2 * SUBTILE
SUBLANES = 8
CONV_PAD = SUBLANES
VMEM_LIMIT = 56 * 1024 * 1024


def _rms(x, g):
    return x * lax.rsqrt(jnp.mean(x * x, axis=-1, keepdims=True) + NORM_EPS) * g


def _lower_head_mask(rows):
    return lax.broadcasted_iota(jnp.int32, (rows, LANES), 1) < HEAD_DIM


def _only_head(x, hh):
    lower_head = _lower_head_mask(x.shape[0])
    half, lower = hh // 2, hh % 2 == 0
    part = x[:, half * LANES:(half + 1) * LANES]
    part = jnp.where(lower_head, part, 0.0) if lower else jnp.where(lower_head, 0.0, part)
    zeros = jnp.zeros_like(part)
    return jnp.concatenate([part if i == half else zeros for i in range(HEADS // 2)], axis=1)


def _to_residue_major(buf, x):
    per = x.shape[0] // RESIDUES
    for half in range(GROUP_W // LANES):
        buf[half] = x[:, half * LANES:(half + 1) * LANES]
    return jnp.concatenate(
        [jnp.concatenate([buf[half, pl.ds(b, per, stride=RESIDUES), :]
                          for half in range(GROUP_W // LANES)], axis=1)
         for b in range(RESIDUES)], axis=0)


def _from_residue_major(buf, x):
    per = SUBTILE // RESIDUES
    for start in range(0, x.shape[0], SUBTILE):
        for b in range(RESIDUES):
            for half in range(GROUP_W // LANES):
                buf[half, pl.ds(start + b, per, stride=RESIDUES), :] = (
                    x[start + b * per:start + (b + 1) * per, half * LANES:(half + 1) * LANES])
    return jnp.concatenate([buf[half] for half in range(GROUP_W // LANES)], axis=1)


def _causal_conv(carry_ref, x, w_ref):
    rows = x.shape[0]
    taps = w_ref.shape[0]
    history = carry_ref[...]
    carry_ref[...] = x[rows - CONV_PAD:, :]
    row = lax.broadcasted_iota(jnp.int32, history.shape, 0)
    out = x * w_ref[taps - 1:taps, :]
    for k in range(taps - 1):
        shift = taps - 1 - k
        rolled = pltpu.roll(x, shift, axis=0)
        head = jnp.where(row < shift, pltpu.roll(history, shift, axis=0), rolled[0:CONV_PAD])
        shifted = jnp.concatenate([head, rolled[CONV_PAD:]], axis=0)
        out += shifted * w_ref[k:k + 1, :]
    return out


def _linear_scan(a, b, h0):
    rows, cols = a.shape
    groups = rows // SUBLANES
    a = a.reshape(groups, SUBLANES, cols)
    b = b.reshape(groups, SUBLANES, cols)
    row = lax.broadcasted_iota(jnp.int32, a.shape, 1)
    d = 1
    while d < SUBLANES:
        keep = row >= d
        a_prev = jnp.where(keep, pltpu.roll(a, d, axis=1), 1.0)
        b_prev = jnp.where(keep, pltpu.roll(b, d, axis=1), 0.0)
        b = a * b_prev + b
        a = a * a_prev
        d *= 2
    out = []
    for g in range(groups):
        hg = b[g] + a[g] * h0
        out.append(hg)
        h0 = hg[SUBLANES - 1:SUBLANES, :]
    return jnp.concatenate(out, axis=0)


def _mixers_abc(z, emit_other_matmuls, prm, y_ref, carry_a, carry_r, carry_h):
    (conv_a_w_ref, conv_r_w_ref, conv_r_b_ref, wa_ref, ba_ref, wx_ref, bx_ref,
     lam_ref, gmlp_g_ref, ws_ref, bs_ref) = prm
    r_x = z(4)
    rows = r_x.shape[0]
    xb = _causal_conv(carry_r, r_x, conv_r_w_ref) + conv_r_b_ref[...]
    xb16 = xb.astype(BF16)
    a_x, a_b, a_c, a_g = z(0), z(1), z(2), z(3)
    r = jax.nn.sigmoid(jnp.dot(xb16, wa_ref[...], preferred_element_type=F32) + ba_ref[...])
    i = jax.nn.sigmoid(jnp.dot(xb16, wx_ref[...], preferred_element_type=F32) + bx_ref[...])
    emit_other_matmuls()
    c_v, c_u, c_g = z(7), z(6), z(8)
    r_g = z(5)

    conv_a = _causal_conv(carry_a, a_c * a_x, conv_a_w_ref)
    y_ref[:, 0:GROUP_W] = (a_b * conv_a * jax.nn.silu(a_g)).astype(y_ref.dtype)

    log_a = (-RG_C * r) * jax.nn.softplus(-lam_ref[...])
    a = jnp.exp(log_a)
    th = jnp.tanh(log_a)
    mult = jnp.sqrt(-2.0 * th / (1.0 - th))
    h = _linear_scan(a, mult * (i * xb), carry_h[0:1, :])
    carry_h[...] = jnp.broadcast_to(h[rows - 1:rows, :], carry_h.shape)
    y_ref[:, GROUP_W:2 * GROUP_W] = (h * jax.nn.silu(r_g)).astype(y_ref.dtype)

    vv = _rms(jax.nn.gelu(c_v), gmlp_g_ref[...])
    t_idx = lax.broadcasted_iota(jnp.int32, ws_ref.shape, 0)
    s_idx = lax.broadcasted_iota(jnp.int32, ws_ref.shape, 1) % GMLP_CHUNK
    ws = jnp.where(s_idx <= t_idx, ws_ref[...], 0.0).astype(BF16)
    for j in range(rows // GMLP_CHUNK):
        sl = slice(j * GMLP_CHUNK, (j + 1) * GMLP_CHUNK)
        vj = vv[sl, :]
        v_heads = jnp.concatenate([_only_head(vj, hh) for hh in range(HEADS)], axis=0).astype(BF16)
        sp = jnp.dot(ws, v_heads, preferred_element_type=F32) + bs_ref[...]
        yc = jax.nn.gelu(c_u[sl, :]) * sp * jax.nn.silu(c_g[sl, :])
        y_ref[sl, 2 * GROUP_W:3 * GROUP_W] = yc.astype(y_ref.dtype)


N_MIX_PARAMS = 11


def _residual(x_ref, yabc_ref, yd_ref, w1_ref, w2_ref, perm_buf):
    y_d = _from_residue_major(perm_buf, yd_ref[...].astype(F32)).astype(BF16)
    y = jnp.concatenate([yabc_ref[...], y_d], axis=1)
    w = jnp.concatenate([w1_ref[...], w2_ref[...]], axis=0)
    return x_ref[...] + jnp.dot(y, w, preferred_element_type=F32)


def _layer_kernel(has_residual, *refs):
    refs = list(refs)
    take = lambda n: [refs.pop(0) for _ in range(n)]
    if has_residual:
        x_ref, yabc_ref, yd_ref, w1_f32, w2_f32 = take(5)
    else:
        (x_ref,) = take(1)
    g_ref, w_f32 = take(2)
    prm = take(N_MIX_PARAMS)
    if has_residual:
        (xo_ref,) = take(1)
    zq_ref, y_ref, carry_a, carry_r, carry_h, perm_buf, w_ref = refs[:7]
    if has_residual:
        w1_ref, w2_ref = refs[7:]

    @pl.when((pl.program_id(0) == 0) & (pl.program_id(1) == 0))
    def _():
        w_ref[...] = w_f32[...].astype(BF16)
        if has_residual:
            w1_ref[...] = w1_f32[...].astype(BF16)
            w2_ref[...] = w2_f32[...].astype(BF16)

    @pl.when(pl.program_id(1) == 0)
    def _():
        carry_a[...] = jnp.zeros_like(carry_a)
        carry_r[...] = jnp.zeros_like(carry_r)
        carry_h[...] = jnp.zeros_like(carry_h)

    if has_residual:
        x = _residual(x_ref, yabc_ref, yd_ref, w1_ref, w2_ref, perm_buf.at[N_CHUNKS - N_MIX_CHUNKS])
        xo_ref[...] = x
    else:
        x = x_ref[...]
    h = _rms(x, g_ref[...]).astype(BF16)

    def z(c):
        cols = slice(c * GROUP_W, (c + 1) * GROUP_W)
        return jnp.dot(h, w_ref[:, cols], preferred_element_type=F32)

    def attention_chunks():
        for k, c in enumerate(range(N_MIX_CHUNKS, N_CHUNKS)):
            zc = z(c) * Q_SCALE if k == 0 else z(c)
            zq_ref[:, k * GROUP_W:(k + 1) * GROUP_W] = _to_residue_major(perm_buf.at[k], zc)

    _mixers_abc(z, attention_chunks, prm, y_ref, carry_a, carry_r, carry_h)


def _resident(a, l):
    return pl.BlockSpec((None,) + a.shape[1:], lambda *_: (l,) + (0,) * (a.ndim - 1),
                        pipeline_mode=pl.Buffered(1))


def _w_out_specs(w_out, l):
    assert D_MIX_ABC % GROUP_W == 0
    return [pl.BlockSpec((None, D_MIX_ABC, D_MODEL), lambda *_: (l, 0, 0),
                         pipeline_mode=pl.Buffered(1)),
            pl.BlockSpec((None, GROUP_W, D_MODEL), lambda *_: (l, D_MIX_ABC // GROUP_W, 0),
                         pipeline_mode=pl.Buffered(1))]


W_OUT_SCRATCH = [pltpu.VMEM((D_MIX_ABC, D_MODEL), BF16), pltpu.VMEM((GROUP_W, D_MODEL), BF16)]


def _layer(batch, seq, l, x2, prev, g, w_in, w_out, prm):
    tiles = seq // LAYER_ROWS
    n = batch * seq
    has_residual = prev is not None

    def rows(width):
        return pl.BlockSpec((LAYER_ROWS, width), lambda b, t: (b * tiles + t, 0))

    def full(a):
        return pl.BlockSpec(a.shape, lambda b, t: (0,) * a.ndim)

    args = [x2] + (list(prev) + [w_out, w_out] if has_residual else []) + [g, w_in] + list(prm)
    in_specs = [rows(D_MODEL)]
    if has_residual:
        in_specs += [rows(D_MIX_ABC), rows(GROUP_W)] + _w_out_specs(w_out, l - 1)
    in_specs += [full(g), _resident(w_in, l)] + [full(a) for a in prm]
    out_specs = [rows(D_ATTN), rows(D_MIX_ABC)]
    out_shape = [jax.ShapeDtypeStruct((n, D_ATTN), F32),
                 jax.ShapeDtypeStruct((n, D_MIX_ABC), BF16)]
    if has_residual:
        out_specs = [rows(D_MODEL)] + out_specs
        out_shape = [jax.ShapeDtypeStruct((n, D_MODEL), F32)] + out_shape
    return pl.pallas_call(
        functools.partial(_layer_kernel, has_residual),
        grid=(batch, tiles),
        in_specs=in_specs,
        out_specs=out_specs,
        out_shape=out_shape,
        scratch_shapes=[
            pltpu.VMEM((CONV_PAD, GROUP_W), F32),
            pltpu.VMEM((CONV_PAD, GROUP_W), F32),
            pltpu.VMEM((SUBLANES, GROUP_W), F32),
            pltpu.VMEM((N_CHUNKS - N_MIX_CHUNKS + 1, GROUP_W // LANES, LAYER_ROWS, LANES), F32),
            pltpu.VMEM((D_MODEL, D_IN), BF16),
        ] + (W_OUT_SCRATCH if has_residual else []),
        compiler_params=pltpu.CompilerParams(
            dimension_semantics=("arbitrary", "arbitrary"), vmem_limit_bytes=VMEM_LIMIT),
        name="layer_res" if has_residual else "layer_first",
    )(*args)


def _out_final_kernel(x_ref, yabc_ref, yd_ref, w1_f32, w2_f32, g_ref, o_ref,
                      perm_buf, w1_ref, w2_ref):
    @pl.when(pl.program_id(0) == 0)
    def _():
        w1_ref[...] = w1_f32[...].astype(BF16)
        w2_ref[...] = w2_f32[...].astype(BF16)

    o_ref[...] = _rms(_residual(x_ref, yabc_ref, yd_ref, w1_ref, w2_ref, perm_buf), g_ref[...])


def _out_final(x2, y_abc, y_d, w_out, l, g):
    n = x2.shape[0]
    rows = lambda width: pl.BlockSpec((FINAL_ROWS, width), lambda i: (i, 0))
    full = lambda a: pl.BlockSpec(a.shape, lambda i: (0,) * a.ndim)
    return pl.pallas_call(
        _out_final_kernel,
        grid=(n // FINAL_ROWS,),
        in_specs=[rows(D_MODEL), rows(D_MIX_ABC), rows(GROUP_W)] + _w_out_specs(w_out, l) + [full(g)],
        out_specs=rows(D_MODEL),
        out_shape=jax.ShapeDtypeStruct((n, D_MODEL), F32),
        scratch_shapes=[pltpu.VMEM((GROUP_W // LANES, FINAL_ROWS, LANES), F32)] + W_OUT_SCRATCH,
        compiler_params=pltpu.CompilerParams(
            dimension_semantics=("arbitrary",), vmem_limit_bytes=VMEM_LIMIT),
        name="out_final",
    )(x2, y_abc, y_d, w_out, w_out, g)


def _attn_kernel(q0, q1, kp0, kp1, kc0, kc1, vp0, vp1, vc0, vc1, g_ref, y_ref,
                 bias_ref, acc_ref, m_ref, l_ref):
    q_ref, kp_ref, kc_ref, vp_ref, vc_ref = (q0, q1), (kp0, kp1), (kc0, kc1), (vp0, vp1), (vc0, vc1)
    first_tile = (pl.program_id(1) == 0).astype(jnp.int32)
    per = SUBTILE // RESIDUES

    def ld(halves, chunks):
        return jnp.concatenate(
            [jnp.concatenate([h[c, :] for h in halves], axis=1) for c, _ in chunks], axis=0)

    def ld_state(ref, chunks):
        return ld((ref.at[0], ref.at[1]), chunks)

    def st_rows(store, chunks, val):
        start = 0
        for c, size in chunks:
            store(c, val[start:start + size, :])
            start += size

    def st_state(ref, chunks, val):
        def store(c, v):
            ref[0, c, :] = v[:, :LANES]
            ref[1, c, :] = v[:, LANES:]
        st_rows(store, chunks, val)

    @pl.when((pl.program_id(0) == 0) & (pl.program_id(1) == 0))
    def _():
        qi = lax.broadcasted_iota(jnp.int32, (ATTN_BLOCK, 2 * ATTN_BLOCK), 0)
        kj = lax.broadcasted_iota(jnp.int32, (ATTN_BLOCK, 2 * ATTN_BLOCK), 1)
        q_major = RESIDUES * (qi % (per // RESIDUES)) + qi // (per // RESIDUES)
        k_major = (RESIDUES * (kj % (2 * per // RESIDUES) - per // RESIDUES)
                   + kj // (2 * per // RESIDUES))
        for p, dil in enumerate(ATTN_DILATIONS):
            steps = q_major - k_major if dil == 1 else qi + ATTN_BLOCK - kj
            in_block = k_major >= 0 if dil == 1 else kj >= ATTN_BLOCK
            band = (steps >= 0) & (steps <= ATTN_BLOCK)
            for hh in range(HEADS):
                slope = 2.0 ** (-8.0 * (hh + 1) / HEADS)
                bias = (-slope * LOG2_E) * (steps * dil).astype(F32)
                bias_ref[(p * HEADS + hh) * 2] = jnp.where(band, bias, MASKED)
                bias_ref[(p * HEADS + hh) * 2 + 1] = jnp.where(band & in_block, bias, MASKED)

    lower_head = _lower_head_mask(ATTN_BLOCK)

    def expand(cols):
        return jnp.concatenate(
            [jnp.where(lower_head, cols[2 * i], cols[2 * i + 1]) for i in range(HEADS // 2)], axis=1)

    def pick_heads(per_head):
        return jnp.concatenate(
            [jnp.where(lower_head, per_head[2 * i][:, i * LANES:(i + 1) * LANES],
                       per_head[2 * i + 1][:, i * LANES:(i + 1) * LANES])
             for i in range(HEADS // 2)], axis=1)

    def run_blocks(mode, specs):
        def scores_of(spec):
            p, first, rows, kk, vv = spec
            q = ld(q_ref, rows)
            q_heads = jnp.concatenate([_only_head(q, hh) for hh in range(HEADS)],
                                      axis=0).astype(BF16)
            return lax.dot_general(q_heads, kk.astype(BF16), (((1,), (1,)), ((), ())),
                                   preferred_element_type=F32)

        def softmax_of(spec, s_all):
            p, first = spec[:2]
            probs, m_cols, l_cols = [], [], []
            for hh in range(HEADS):
                s = s_all[hh * ATTN_BLOCK:(hh + 1) * ATTN_BLOCK, :]
                s = s + bias_ref[(p * HEADS + hh) * 2 + first]
                m_h = jnp.max(s, axis=-1, keepdims=True)
                e = jnp.exp2(s - m_h)
                probs.append(e.astype(BF16))
                m_cols.append(m_h)
                l_cols.append(jnp.sum(e, axis=-1, keepdims=True))
            return jnp.concatenate(probs, axis=0), m_cols, l_cols

        def values_of(spec, p_all, m_cols, l_cols):
            p, first, rows, kk, vv = spec
            o_all = jnp.dot(p_all, vv.astype(BF16), preferred_element_type=F32)
            acc = pick_heads([o_all[hh * ATTN_BLOCK:(hh + 1) * ATTN_BLOCK, :] for hh in range(HEADS)])
            m_b = expand(m_cols)
            l_b = expand(l_cols)
            if mode != "init":
                m_s, l_s, acc_s = ld_state(m_ref, rows), ld_state(l_ref, rows), ld_state(acc_ref, rows)
                m_n = jnp.maximum(m_s, m_b)
                w_s = jnp.exp2(m_s - m_n)
                w_b = jnp.exp2(m_b - m_n)
                acc = acc_s * w_s + acc * w_b
                l_b = l_s * w_s + l_b * w_b
                m_b = m_n
            if mode == "final":
                gate = jnp.concatenate([g_ref[c, :] for c, _ in rows], axis=0)
                out = (acc / l_b * jax.nn.silu(gate)).astype(y_ref.dtype)

                def store(c, v):
                    y_ref[c, :] = v
                st_rows(store, rows, out)
            else:
                st_state(m_ref, rows, m_b)
                st_state(l_ref, rows, l_b)
                st_state(acc_ref, rows, acc)

        scores = {i: scores_of(specs[i]) for i in range(min(SCORE_LOOKAHEAD, len(specs)))}
        for i, spec in enumerate(specs):
            soft = softmax_of(spec, scores.pop(i))
            if i + SCORE_LOOKAHEAD < len(specs):
                scores[i + SCORE_LOOKAHEAD] = scores_of(specs[i + SCORE_LOOKAHEAD])
            values_of(spec, *soft)

    def keys_values(prev_refs, prev_rows, rows):
        k_src, v_src = (kp_ref, vp_ref) if prev_refs else (kc_ref, vc_ref)
        kk = jnp.concatenate([ld(k_src, prev_rows), ld(kc_ref, rows)], axis=0)
        vv = jnp.concatenate([ld(v_src, prev_rows), ld(vc_ref, rows)], axis=0)
        return kk, vv


    def spec16(r):
        a, b = r // RESIDUES, r % RESIDUES
        rows = [(pl.ds(t * SUBTILE + b * per + a, per // RESIDUES, stride=RESIDUES), per // RESIDUES)
                for t in range(ATTN_TILE // SUBTILE)]
        return (2, first_tile, rows) + keys_values(1, rows, rows)

    residues16 = max(ATTN_DILATIONS)
    for r0 in range(0, residues16, GROUP_STRIDE16):
        run_blocks("init", [spec16(r) for r in range(r0, r0 + GROUP_STRIDE16)])

    def spec4(t, b):
        rows = [(pl.ds(t * SUBTILE + b * per, per), per)]
        t_prev = (t - 1) % (ATTN_TILE // SUBTILE)
        prev_rows = [(pl.ds(t_prev * SUBTILE + b * per, per), per)]
        first = first_tile if t == 0 else 0
        return (1, first, rows) + keys_values(t == 0, prev_rows, rows)

    for t0 in range(0, ATTN_TILE // SUBTILE, SUBTILES_PER_GROUP):
        run_blocks("merge", [spec4(t, b) for t in range(t0, t0 + SUBTILES_PER_GROUP)
                             for b in range(RESIDUES)])

    n_sub = SUBTILE // ATTN_BLOCK
    blk = per // n_sub

    def spec1(t, u):
        base = t * SUBTILE
        rows = [(pl.ds(base + b * per + u * blk, blk), blk) for b in range(RESIDUES)]
        if u > 0:
            key_rows = [(pl.ds(base + b * per + (u - 1) * blk, 2 * blk), 2 * blk)
                        for b in range(RESIDUES)]
            return (0, 0, rows, ld(kc_ref, key_rows), ld(vc_ref, key_rows))
        first_of_tile = t == 0
        k_src, v_src = (kp_ref, vp_ref) if first_of_tile else (kc_ref, vc_ref)
        prev_base = (ATTN_TILE - SUBTILE) if first_of_tile else base - SUBTILE
        kk, vv = [], []
        for b in range(RESIDUES):
            before = [(pl.ds(prev_base + b * per + per - blk, blk), blk)]
            kk += [ld(k_src, before), ld(kc_ref, rows[b:b + 1])]
            vv += [ld(v_src, before), ld(vc_ref, rows[b:b + 1])]
        first = first_tile if first_of_tile else 0
        return (0, first, rows, jnp.concatenate(kk, axis=0), jnp.concatenate(vv, axis=0))

    for t0 in range(0, ATTN_TILE // SUBTILE, SUBTILES_PER_GROUP):
        run_blocks("final", [spec1(t, u) for t in range(t0, t0 + SUBTILES_PER_GROUP)
                             for u in range(n_sub)])


def _attention(zq, batch, seq):
    tiles = seq // ATTN_TILE
    halves = GROUP_W // LANES

    def cur(c):
        return [pl.BlockSpec((ATTN_TILE, LANES), lambda b, n, c=c, h=h: (b * tiles + n, c * halves + h))
                for h in range(halves)]

    def prev(c):
        return [pl.BlockSpec((ATTN_TILE, LANES),
                             lambda b, n, c=c, h=h: (b * tiles + jnp.maximum(n - 1, 0), c * halves + h))
                for h in range(halves)]

    in_specs = cur(0) + prev(1) + cur(1) + prev(2) + cur(2) + [
        pl.BlockSpec((ATTN_TILE, GROUP_W), lambda b, n: (b * tiles + n, 3))]
    state = pltpu.VMEM((halves, ATTN_TILE, LANES), F32)
    return pl.pallas_call(
        _attn_kernel,
        grid=(batch, tiles),
        in_specs=in_specs,
        out_specs=pl.BlockSpec((ATTN_TILE, GROUP_W), lambda b, n: (b * tiles + n, 0)),
        out_shape=jax.ShapeDtypeStruct((batch * seq, GROUP_W), BF16),
        scratch_shapes=[
            pltpu.VMEM((len(ATTN_DILATIONS) * HEADS * 2, ATTN_BLOCK, 2 * ATTN_BLOCK), F32),
            state, state, state,
        ],
        compiler_params=pltpu.CompilerParams(
            dimension_semantics=("arbitrary", "arbitrary"), vmem_limit_bytes=VMEM_LIMIT),
        name="dilated_attention",
    )(*([zq] * len(in_specs)))


def _block_diag(w):
    h, d, _ = w.shape
    eye = jnp.eye(h, dtype=w.dtype)
    return jnp.einsum("hij,hg->higj", w, eye).reshape(h * d, h * d)


def kernel(x, norm_g, w_in, conv_a_w, conv_r_w, conv_r_b, lru_wa, lru_ba, lru_wx, lru_bx,
           lru_lambda, gmlp_norm_g, gmlp_ws, gmlp_bs, w_out, final_g):
    batch, seq, _ = x.shape
    depth = w_in.shape[0]
    x2 = x.reshape(batch * seq, D_MODEL)
    row = lambda v: v.reshape(1, -1)
    prev = None
    for l in range(depth):
        prm = [
            conv_a_w[l], conv_r_w[l], row(conv_r_b[l]),
            _block_diag(lru_wa[l]).astype(BF16), row(lru_ba[l]),
            _block_diag(lru_wx[l]).astype(BF16), row(lru_bx[l]),
            row(lru_lambda[l]), row(gmlp_norm_g[l]),
            jnp.transpose(gmlp_ws[l], (1, 0, 2)).reshape(GMLP_CHUNK, HEADS * GMLP_CHUNK),
            jnp.repeat(jnp.transpose(gmlp_bs[l]), HEAD_DIM, axis=1),
        ]
        outs = _layer(batch, seq, l, x2, prev, row(norm_g[l]), w_in, w_out, prm)
        if prev is not None:
            x2 = outs[0]
        zq, y_abc = outs[-2:]
        prev = (y_abc, _attention(zq, batch, seq))
    out = _out_final(x2, *prev, w_out, depth - 1, row(final_g))
    return out.reshape(batch, seq, D_MODEL)
```
